```python
import jax, jax.numpy as jnp
from jax import lax
import numpy as np

D_MODEL = 1024
BATCH = 16
SEQ = 2048
DEPTH = 1

D_RNN = D_MODEL
RNN_BLOCKS = 16
RNN_BLOCK_W = D_RNN // RNN_BLOCKS
CONV_W = 4
LRU_C = 8.0
HEAD_DIM = 64
N_Q_HEADS = D_MODEL // HEAD_DIM
N_KV_HEADS = 4
GQA_GROUP = N_Q_HEADS // N_KV_HEADS
WINDOW = 128
ATTN_BLOCK = WINDOW
ROPE_THETA = 10000.0
Q_W = N_Q_HEADS * HEAD_DIM
KV_W = N_KV_HEADS * HEAD_DIM
D_FF = 4 * D_MODEL
PLE_DIM = 256
NORM_EPS = 1e-6
IN_WIDTHS = [D_RNN, D_RNN, Q_W, KV_W, KV_W, D_MODEL, D_MODEL]
IN_TOTAL = int(sum(IN_WIDTHS))
SPLIT_IDX = [int(v) for v in np.cumsum(IN_WIDTHS)[:-1]]

kernel_name = 'hybrid_rglru_swa_sink_gated_block'


def _rmsnorm(t, g):
    tf = t.astype(jnp.float32)
    y = tf * lax.rsqrt(jnp.mean(tf * tf, axis=-1, keepdims=True) + NORM_EPS)
    return (y * g.astype(jnp.float32)).astype(t.dtype)


def _rope_tables(S):
    inv = ROPE_THETA ** (-jnp.arange(0, HEAD_DIM, 2, dtype=jnp.float32) / HEAD_DIM)
    ang = jnp.arange(S, dtype=jnp.float32)[:, None] * inv[None, :]
    return jnp.cos(ang), jnp.sin(ang)


def _rope(t, cos, sin):
    tf = t.astype(jnp.float32)
    t1, t2 = jnp.split(tf, 2, axis=-1)
    c = cos[None, :, None, :]
    s = sin[None, :, None, :]
    return jnp.concatenate([t1 * c - t2 * s, t2 * c + t1 * s], axis=-1).astype(t.dtype)


def _causal_conv(t, w, b):
    S = t.shape[1]
    tp = jnp.pad(t, ((0, 0), (CONV_W - 1, 0), (0, 0)))
    out = b + tp[:, 0:S] * w[0]
    for j in range(1, CONV_W):
        out = out + tp[:, j:j + S] * w[j]
    return out


def _rg_lru(xc, w_rg, b_rg, w_ig, b_ig, lam):
    B, S, _ = xc.shape
    xb = xc.reshape(B, S, RNN_BLOCKS, RNN_BLOCK_W)
    r = jax.nn.sigmoid(jnp.einsum('bshi,hij->bshj', xb, w_rg).reshape(B, S, D_RNN) + b_rg)
    i = jax.nn.sigmoid(jnp.einsum('bshi,hij->bshj', xb, w_ig).reshape(B, S, D_RNN) + b_ig)
    log_a = -LRU_C * r.astype(jnp.float32) * jax.nn.softplus(-lam.astype(jnp.float32))
    a = jnp.exp(log_a)
    mult = jnp.sqrt(-jnp.expm1(2.0 * log_a))
    bterm = mult * (i * xc).astype(jnp.float32)

    def combine(left, right):
        a1, b1 = left
        a2, b2 = right
        return a1 * a2, a2 * b1 + b2

    _, h = lax.associative_scan(combine, (a, bterm), axis=1)
    return h.astype(xc.dtype)


def _sliding_window_attention(q, k, v, q_gain, k_gain, sinks, cos, sin):
    B, S, _ = q.shape
    NB = S // ATTN_BLOCK
    q = _rope(_rmsnorm(q.reshape(B, S, N_Q_HEADS, HEAD_DIM), q_gain), cos, sin)
    k = _rope(_rmsnorm(k.reshape(B, S, N_KV_HEADS, HEAD_DIM), k_gain), cos, sin)
    v = v.reshape(B, S, N_KV_HEADS, HEAD_DIM)
    qb = q.reshape(B, NB, ATTN_BLOCK, N_KV_HEADS, GQA_GROUP, HEAD_DIM)

    def band(t):
        tb = t.reshape(B, NB, ATTN_BLOCK, N_KV_HEADS, HEAD_DIM)
        prev = jnp.pad(tb[:, :-1], ((0, 0), (1, 0), (0, 0), (0, 0), (0, 0)))
        return jnp.concatenate([prev, tb], axis=2)

    kb = band(k)
    vb = band(v)
    s = jnp.einsum('bnqkgd,bnckd->bnkgqc', qb, kb).astype(jnp.float32) * (HEAD_DIM ** -0.5)
    qi = jnp.arange(ATTN_BLOCK)[:, None]
    ci = jnp.arange(2 * ATTN_BLOCK)[None, :]
    diff = ATTN_BLOCK + qi - ci
    blk = jnp.arange(NB)[:, None, None]
    valid = (diff >= 0) & (diff < WINDOW) & ((blk - 1) * ATTN_BLOCK + ci >= 0)
    s = jnp.where(valid[None, :, None, None, :, :], s, -jnp.inf)
    sink = sinks.astype(jnp.float32).reshape(N_KV_HEADS, GQA_GROUP)[None, None, :, :, None, None]
    m = jnp.maximum(jnp.max(s, axis=-1, keepdims=True), sink)
    e = jnp.exp(s - m)
    probs = e / (jnp.sum(e, axis=-1, keepdims=True) + jnp.exp(sink - m))
    o = jnp.einsum('bnkgqc,bnckd->bnqkgd', probs.astype(v.dtype), vb)
    return o.reshape(B, S, Q_W)


def setup_inputs(seed: int = 0) -> dict:
    key = jax.random.key(seed)
    ks = jax.random.split(key, 24)
    f32 = jnp.float32
    L = DEPTH

    def nrm(k, shape, scale):
        return jax.random.normal(k, shape, f32) * scale

    u = jax.random.uniform(ks[10], (L, D_RNN), f32, minval=0.9, maxval=0.999)
    s_a = u ** (1.0 / LRU_C)
    lru_lambda = jnp.log(s_a) - jnp.log1p(-s_a)
    return {
        'x': nrm(ks[0], (BATCH, SEQ, D_MODEL), 1.0),
        'p': nrm(ks[1], (DEPTH, BATCH, SEQ, PLE_DIM), 1.0),
        'g_mix': 1.0 + nrm(ks[2], (L, D_MODEL), 0.02),
        'w_in': nrm(ks[3], (L, D_MODEL, IN_TOTAL), D_MODEL ** -0.5),
        'conv_w': nrm(ks[4], (L, CONV_W, D_RNN), CONV_W ** -0.5),
        'conv_b': nrm(ks[5], (L, D_RNN), 0.01),
        'w_rg': nrm(ks[6], (L, RNN_BLOCKS, RNN_BLOCK_W, RNN_BLOCK_W), RNN_BLOCK_W ** -0.5),
        'b_rg': nrm(ks[7], (L, D_RNN), 0.01),
        'w_ig': nrm(ks[8], (L, RNN_BLOCKS, RNN_BLOCK_W, RNN_BLOCK_W), RNN_BLOCK_W ** -0.5),
        'b_ig': nrm(ks[9], (L, D_RNN), 0.01),
        'lru_lambda': lru_lambda,
        'w_rnn_proj': nrm(ks[11], (L, D_RNN, D_MODEL), D_RNN ** -0.5),
        'q_gain': 1.0 + nrm(ks[12], (L, HEAD_DIM), 0.02),
        'k_gain': 1.0 + nrm(ks[13], (L, HEAD_DIM), 0.02),
        'sinks': nrm(ks[14], (L, N_Q_HEADS), 0.5),
        'w_attn_proj': nrm(ks[15], (L, Q_W, D_MODEL), Q_W ** -0.5),
        'w_out': nrm(ks[16], (L, D_MODEL, D_MODEL), D_MODEL ** -0.5),
        'g_mlp': 1.0 + nrm(ks[17], (L, D_MODEL), 0.02),
        'w_up': nrm(ks[18], (L, D_MODEL, D_FF), D_MODEL ** -0.5),
        'w_down': nrm(ks[19], (L, D_FF, D_MODEL), D_FF ** -0.5),
        'g_ple': 1.0 + nrm(ks[20], (L, D_MODEL), 0.02),
        'w_ple_gate': nrm(ks[21], (L, D_MODEL, D_MODEL), D_MODEL ** -0.5),
        'w_ple_proj': nrm(ks[22], (L, PLE_DIM, D_MODEL), PLE_DIM ** -0.5),
    }


def reference(x, p, g_mix, w_in, conv_w, conv_b, w_rg, b_rg, w_ig, b_ig, lru_lambda,
              w_rnn_proj, q_gain, k_gain, sinks, w_attn_proj, w_out, g_mlp, w_up, w_down,
              g_ple, w_ple_gate, w_ple_proj):
    S = x.shape[1]
    cos, sin = _rope_tables(S)
    for l in range(DEPTH):
        h = _rmsnorm(x, g_mix[l])
        z = h @ w_in[l]
        x_rnn, g_rnn, q, k, v, gate_a, gate_b = jnp.split(z, SPLIT_IDX, axis=-1)
        xc = _causal_conv(x_rnn, conv_w[l], conv_b[l])
        hr = _rg_lru(xc, w_rg[l], b_rg[l], w_ig[l], b_ig[l], lru_lambda[l])
        y_a = (hr * jax.nn.gelu(g_rnn)) @ w_rnn_proj[l]
        y_b = _sliding_window_attention(q, k, v, q_gain[l], k_gain[l], sinks[l], cos, sin) @ w_attn_proj[l]
        merged = jax.nn.sigmoid(gate_a) * y_a + jax.nn.sigmoid(gate_b) * y_b
        x = x + merged @ w_out[l]
        hm = _rmsnorm(x, g_mlp[l])
        x = x + jnp.square(jax.nn.relu(hm @ w_up[l])) @ w_down[l]
        e = p[l] @ w_ple_proj[l]
        x = x + e * jax.nn.sigmoid(_rmsnorm(x, g_ple[l]) @ w_ple_gate[l])
    return x
```

```python
import functools
import math

import jax
import jax.numpy as jnp
from jax import lax
from jax.experimental import pallas as pl
from jax.experimental.pallas import tpu as pltpu

D_MODEL = 1024
D_RNN = D_MODEL
RNN_BLOCKS = 16
RNN_BLOCK_W = D_RNN // RNN_BLOCKS
CONV_W = 4
LRU_C = 8.0
HEAD_DIM = 64
HALF_DIM = HEAD_DIM // 2
N_Q_HEADS = D_MODEL // HEAD_DIM
N_KV_HEADS = 4
GQA_GROUP = N_Q_HEADS // N_KV_HEADS
WINDOW = 128
ROPE_THETA = 10000.0
Q_W = N_Q_HEADS * HEAD_DIM
KV_W = N_KV_HEADS * HEAD_DIM
D_FF = 4 * D_MODEL
PLE_DIM = 256
NORM_EPS = 1e-6

OFF_XR = 0
OFF_GR = OFF_XR + D_RNN
OFF_Q = OFF_GR + D_RNN
OFF_K = OFF_Q + Q_W
OFF_V = OFF_K + KV_W
OFF_GA = OFF_V + KV_W
OFF_GB = OFF_GA + D_MODEL
IN_TOTAL = OFF_GB + D_MODEL

V7X_LANES = 128
V7X_SUBLANES = 8
V7X_MXU_DIM = 256
V7X_VMEM_BYTES = 64 * 1024 * 1024

GATE_TILES = D_RNN // V7X_MXU_DIM
SEQ_CHUNK = 256
FFN_ROWS = 512
FF_CHUNK = 1024
VMEM_LIMIT = V7X_VMEM_BYTES * 3 // 4

BF16 = jnp.bfloat16
F32 = jnp.float32


def _dot(a, b):
    return jnp.dot(a, b, preferred_element_type=F32)


def _sigmoid(t):
    return 1.0 / (1.0 + jnp.exp(-t))


def _rmsnorm(t, g):
    ms = jnp.mean(t * t, axis=-1, keepdims=True)
    return t * lax.rsqrt(ms + NORM_EPS) * g


def _mixer_kernel(sinks_ref, x_ref, w_in_ref, g_mix_ref, conv_w_ref, conv_b_ref, wgate_ref,
                  b_rg_ref, b_ig_ref, lam_ref, w_rnn_ref, ones_ref, qgain_ref, kgain_ref,
                  cos_ref, sa_ref, sb_ref, bias_ref, w_attn_ref, w_out_ref,
                  o_ref,
                  ext_ref, a_ref, b_ref, hr_ref, hcar_ref, kext_ref, vext_ref, ao_ref):
    ts = SEQ_CHUNK
    j = pl.program_id(1)

    @pl.when(j == 0)
    def _():
        ext_ref[0:V7X_SUBLANES, :] = jnp.zeros((V7X_SUBLANES, D_RNN), F32)
        hcar_ref[...] = jnp.zeros_like(hcar_ref)
        kext_ref[0:WINDOW, :] = jnp.zeros((WINDOW, KV_W), BF16)
        vext_ref[0:WINDOW, :] = jnp.zeros((WINDOW, KV_W), BF16)

    x = x_ref[0]
    h = _rmsnorm(x, g_mix_ref[...]).astype(BF16)

    def proj(lo, width):
        return _dot(h, w_in_ref[:, lo:lo + width])

    xr = proj(OFF_XR, D_RNN)
    ext_ref[V7X_SUBLANES:V7X_SUBLANES + ts, :] = xr
    cw = conv_w_ref[...]
    xc = conv_b_ref[...] + xr * cw[CONV_W - 1:CONV_W]
    for jj in range(CONV_W - 1):
        start = V7X_SUBLANES - (CONV_W - 1) + jj
        xc = xc + ext_ref[start:start + ts, :] * cw[jj:jj + 1]
    ext_ref[0:V7X_SUBLANES, :] = xr[ts - V7X_SUBLANES:ts]

    lam = lam_ref[...]
    neg_c_sp = -LRU_C * (jnp.maximum(-lam, 0.0) + jnp.log1p(jnp.exp(-jnp.abs(lam))))
    for c in range(GATE_TILES):
        sl = slice(c * V7X_MXU_DIM, (c + 1) * V7X_MXU_DIM)
        xcc = xc[:, sl]
        gz = _dot(xcc.astype(BF16), wgate_ref[c])
        r = _sigmoid(gz[:, :V7X_MXU_DIM] + b_rg_ref[:, sl])
        i = _sigmoid(gz[:, V7X_MXU_DIM:] + b_ig_ref[:, sl])
        log_a = neg_c_sp[:, sl] * r
        a = jnp.exp(log_a)
        a_ref[:, sl] = a
        b_ref[:, sl] = jnp.sqrt(-jnp.tanh(log_a) * (a * a + 1.0)) * (i * xcc)

    def scan_body(blk, hc):
        base = pl.multiple_of(blk * V7X_SUBLANES, V7X_SUBLANES)
        for rr in range(V7X_SUBLANES):
            hc = a_ref[pl.ds(base + rr, 1), :] * hc + b_ref[pl.ds(base + rr, 1), :]
            hr_ref[pl.ds(base + rr, 1), :] = hc
        return hc

    hcar_ref[...] = lax.fori_loop(0, ts // V7X_SUBLANES, scan_body, hcar_ref[...])

    gr = proj(OFF_GR, D_RNN)
    gelu = gr * (0.5 * (1.0 + jnp.tanh(math.sqrt(2.0 / math.pi) * (gr + 0.044715 * (gr * gr * gr)))))
    y_a = _dot((hr_ref[...] * gelu).astype(BF16), w_rnn_ref[...])

    def headnorm(z, gain):
        zz = (z * z).astype(BF16)
        parts = [_dot(zz[:, c * V7X_MXU_DIM:(c + 1) * V7X_MXU_DIM], ones_ref[...])
                 for c in range(z.shape[1] // V7X_MXU_DIM)]
        ss = parts[0] if len(parts) == 1 else jnp.concatenate(parts, axis=1)
        return z * lax.rsqrt(ss * (1.0 / HEAD_DIM) + NORM_EPS) * gain

    cos_t, sa_t, sb_t = cos_ref[...], sa_ref[...], sb_ref[...]

    def rope(t):
        outs = []
        for c in range(t.shape[1] // V7X_LANES):
            tc = t[:, c * V7X_LANES:(c + 1) * V7X_LANES]
            outs.append(tc * cos_t
                        + pltpu.roll(tc, V7X_LANES - HALF_DIM, 1) * sa_t
                        + pltpu.roll(tc, HALF_DIM, 1) * sb_t)
        return jnp.concatenate(outs, axis=1)

    qb = rope(headnorm(proj(OFF_Q, Q_W), qgain_ref[...])).astype(BF16)
    kext_ref[WINDOW:WINDOW + ts, :] = rope(headnorm(proj(OFF_K, KV_W), kgain_ref[...])).astype(BF16)
    vext_ref[WINDOW:WINDOW + ts, :] = proj(OFF_V, KV_W).astype(BF16)

    for n in range(ts // WINDOW):
        rows = slice(n * WINDOW, (n + 1) * WINDOW)
        bias = bias_ref[1] if n > 0 else bias_ref[jnp.minimum(j, 1)]
        q_blk = qb[rows]
        for g in range(N_KV_HEADS):
            heads = [g * GQA_GROUP + i for i in range(GQA_GROUP)]
            q_g = jnp.concatenate([q_blk[:, hd * HEAD_DIM:(hd + 1) * HEAD_DIM] for hd in heads], axis=0)
            k_g = kext_ref[n * WINDOW:(n + 2) * WINDOW, g * HEAD_DIM:(g + 1) * HEAD_DIM]
            v_g = vext_ref[n * WINDOW:(n + 2) * WINDOW, g * HEAD_DIM:(g + 1) * HEAD_DIM]
            s = lax.dot_general(q_g, k_g, (((1,), (1,)), ((), ())), preferred_element_type=F32)
            probs, dens = [], []
            for i, hd in enumerate(heads):
                s_i = s[i * WINDOW:(i + 1) * WINDOW] + bias
                sink = sinks_ref[hd]
                m = jnp.maximum(jnp.max(s_i, axis=-1, keepdims=True), sink)
                e = jnp.exp(s_i - m)
                dens.append(jnp.sum(e, axis=-1, keepdims=True) + jnp.exp(sink - m))
                probs.append(e.astype(BF16))
            o = _dot(jnp.concatenate(probs, axis=0), v_g)
            for i, hd in enumerate(heads):
                ao_ref[rows, hd * HEAD_DIM:(hd + 1) * HEAD_DIM] = o[i * WINDOW:(i + 1) * WINDOW] / dens[i]
    kext_ref[0:WINDOW, :] = kext_ref[ts:ts + WINDOW, :]
    vext_ref[0:WINDOW, :] = vext_ref[ts:ts + WINDOW, :]
    y_b = _dot(ao_ref[...].astype(BF16), w_attn_ref[...])

    g_a = _sigmoid(proj(OFF_GA, D_MODEL))
    g_b = _sigmoid(proj(OFF_GB, D_MODEL))
    merged = (g_a * y_a + g_b * y_b).astype(BF16)
    o_ref[0] = x + _dot(merged, w_out_ref[...])


def _ffn_kernel(x_ref, p_ref, g_mlp_ref, w_up_ref, w_down_ref, g_ple_ref, w_pg_ref, w_pp_ref, o_ref):
    x = x_ref[...]
    hm = _rmsnorm(x, g_mlp_ref[...]).astype(BF16)
    acc = x
    for c in range(D_FF // FF_CHUNK):
        u = jnp.maximum(_dot(hm, w_up_ref[:, c * FF_CHUNK:(c + 1) * FF_CHUNK]), 0.0)
        acc = acc + _dot((u * u).astype(BF16), w_down_ref[c * FF_CHUNK:(c + 1) * FF_CHUNK, :])
    hp = _rmsnorm(acc, g_ple_ref[...]).astype(BF16)
    gate = _sigmoid(_dot(hp, w_pg_ref[...]))
    e = _dot(p_ref[...].astype(BF16), w_pp_ref[...])
    o_ref[...] = acc + e * gate


def _resident(shape):
    nd = len(shape)
    return pl.BlockSpec(shape, lambda *_: (0,) * nd, pipeline_mode=pl.Buffered(1))


def _block_diag_tiles(w):
    per = V7X_MXU_DIM // RNN_BLOCK_W
    w4 = w.reshape(GATE_TILES, per, RNN_BLOCK_W, RNN_BLOCK_W)
    eye = jnp.eye(per, dtype=w.dtype)
    return jnp.einsum('cgij,gh->cgihj', w4, eye).reshape(GATE_TILES, V7X_MXU_DIM, V7X_MXU_DIM)


def _rope_tables(seq):
    inv = ROPE_THETA ** (-jnp.arange(0, HEAD_DIM, 2, dtype=F32) / HEAD_DIM)
    ang = jnp.arange(seq, dtype=F32)[:, None] * inv[None, :]
    cos, sin = jnp.cos(ang), jnp.sin(ang)
    reps = V7X_LANES // HALF_DIM
    first_half = (jnp.arange(V7X_LANES) % HEAD_DIM) < HALF_DIM
    cos_t = jnp.tile(cos, (1, reps))
    sin_t = jnp.tile(sin, (1, reps))
    return cos_t, jnp.where(first_half, -sin_t, 0.0), jnp.where(first_half, 0.0, sin_t)


def _band_bias():
    qi = jnp.arange(WINDOW)[:, None]
    ci = jnp.arange(2 * WINDOW)[None, :]
    diff = WINDOW + qi - ci
    valid = (diff >= 0) & (diff < WINDOW)
    first = valid & (ci >= WINDOW)
    neg = jnp.float32(-jnp.inf)
    return jnp.stack([jnp.where(first, 0.0, neg), jnp.where(valid, 0.0, neg)]).astype(F32)


def _mixer(x, sinks, mats, vecs):
    batch, seq, _ = x.shape
    ts = SEQ_CHUNK
    assert seq % ts == 0 and ts % WINDOW == 0
    (w_in, wgate, w_rnn, ones, w_attn, w_out) = mats
    (g_mix, conv_w, conv_b, b_rg, b_ig, lam, qgain, kgain, cos_t, sa_t, sb_t, bias) = vecs
    rope_spec = pl.BlockSpec((ts, V7X_LANES), lambda b, j, *_: (j, 0))
    in_specs = [
        pl.BlockSpec((1, ts, D_MODEL), lambda b, j, *_: (b, j, 0)),
        _resident(w_in.shape), _resident(g_mix.shape), _resident(conv_w.shape), _resident(conv_b.shape),
        _resident(wgate.shape), _resident(b_rg.shape), _resident(b_ig.shape), _resident(lam.shape),
        _resident(w_rnn.shape), _resident(ones.shape), _resident(qgain.shape), _resident(kgain.shape),
        rope_spec, rope_spec, rope_spec, _resident(bias.shape), _resident(w_attn.shape),
        _resident(w_out.shape),
    ]
    grid_spec = pltpu.PrefetchScalarGridSpec(
        num_scalar_prefetch=1,
        grid=(batch, seq // ts),
        in_specs=in_specs,
        out_specs=pl.BlockSpec((1, ts, D_MODEL), lambda b, j, *_: (b, j, 0)),
        scratch_shapes=[
            pltpu.VMEM((ts + V7X_SUBLANES, D_RNN), F32),
            pltpu.VMEM((ts, D_RNN), F32),
            pltpu.VMEM((ts, D_RNN), F32),
            pltpu.VMEM((ts, D_RNN), F32),
            pltpu.VMEM((1, D_RNN), F32),
            pltpu.VMEM((ts + WINDOW, KV_W), BF16),
            pltpu.VMEM((ts + WINDOW, KV_W), BF16),
            pltpu.VMEM((ts, Q_W), F32),
        ],
    )
    return pl.pallas_call(
        _mixer_kernel,
        grid_spec=grid_spec,
        out_shape=jax.ShapeDtypeStruct(x.shape, F32),
        compiler_params=pltpu.CompilerParams(
            dimension_semantics=("arbitrary", "arbitrary"), vmem_limit_bytes=VMEM_LIMIT),
        name="mixer",
    )(sinks, x, w_in, g_mix, conv_w, conv_b, wgate, b_rg, b_ig, lam, w_rnn, ones, qgain, kgain,
      cos_t, sa_t, sb_t, bias, w_attn, w_out)


def _ffn(x2d, p2d, g_mlp, w_up, w_down, g_ple, w_pg, w_pp):
    rows = x2d.shape[0]
    tm = FFN_ROWS
    assert rows % tm == 0
    return pl.pallas_call(
        _ffn_kernel,
        grid=(rows // tm,),
        in_specs=[
            pl.BlockSpec((tm, D_MODEL), lambda i: (i, 0)),
            pl.BlockSpec((tm, PLE_DIM), lambda i: (i, 0)),
            _resident(g_mlp.shape), _resident(w_up.shape), _resident(w_down.shape),
            _resident(g_ple.shape), _resident(w_pg.shape), _resident(w_pp.shape),
        ],
        out_specs=pl.BlockSpec((tm, D_MODEL), lambda i: (i, 0)),
        out_shape=jax.ShapeDtypeStruct(x2d.shape, F32),
        compiler_params=pltpu.CompilerParams(
            dimension_semantics=("arbitrary",), vmem_limit_bytes=VMEM_LIMIT),
        name="ffn",
    )(x2d, p2d, g_mlp, w_up, w_down, g_ple, w_pg, w_pp)


def kernel(x, p, g_mix, w_in, conv_w, conv_b, w_rg, b_rg, w_ig, b_ig, lru_lambda, w_rnn_proj, q_gain, k_gain, sinks, w_attn_proj, w_out, g_mlp, w_up, w_down, g_ple, w_ple_gate, w_ple_proj):
    batch, seq, _ = x.shape
    depth = p.shape[0]
    cos_t, sa_t, sb_t = _rope_tables(seq)
    bias = _band_bias()
    per = V7X_MXU_DIM // HEAD_DIM
    ones = jnp.kron(jnp.eye(per, dtype=F32), jnp.ones((HEAD_DIM, HEAD_DIM), F32)).astype(BF16)
    row = lambda v: v.reshape(1, -1)
    for l in range(depth):
        wgate = jnp.concatenate([_block_diag_tiles(w_rg[l]), _block_diag_tiles(w_ig[l])], axis=2).astype(BF16)
        mats = (w_in[l].astype(BF16), wgate, w_rnn_proj[l].astype(BF16), ones,
                w_attn_proj[l].astype(BF16), w_out[l].astype(BF16))
        vecs = (row(g_mix[l]), conv_w[l], row(conv_b[l]), row(b_rg[l]), row(b_ig[l]), row(lru_lambda[l]),
                row(jnp.tile(q_gain[l] * HEAD_DIM ** -0.5, N_Q_HEADS)), row(jnp.tile(k_gain[l], N_KV_HEADS)),
                cos_t, sa_t, sb_t, bias)
        x = _mixer(x, sinks[l], mats, vecs)
        x = _ffn(x.reshape(batch * seq, D_MODEL), p[l].reshape(batch * seq, PLE_DIM), row(g_mlp[l]),
                 w_up[l].astype(BF16), w_down[l].astype(BF16), row(g_ple[l]),
                 w_ple_gate[l].astype(BF16), w_ple_proj[l].astype(BF16)).reshape(batch, seq, D_MODEL)
    return x
```

```python
import math

import jax
import jax.numpy as jnp
from jax import lax
from jax.experimental import pallas as pl
from jax.experimental.pallas import tpu as pltpu

D_MODEL = 1024
D_RNN = D_MODEL
RNN_BLOCKS = 16
RNN_BLOCK_W = D_RNN // RNN_BLOCKS
CONV_W = 4
LRU_C = 8.0
HEAD_DIM = 64
HALF_DIM = HEAD_DIM // 2
N_Q_HEADS = D_MODEL // HEAD_DIM
N_KV_HEADS = 4
GQA_GROUP = N_Q_HEADS // N_KV_HEADS
WINDOW = 128
ROPE_THETA = 10000.0
Q_W = N_Q_HEADS * HEAD_DIM
KV_W = N_KV_HEADS * HEAD_DIM
QKV_W = Q_W + 2 * KV_W
D_FF = 4 * D_MODEL
PLE_DIM = 256
NORM_EPS = 1e-6

OFF_XR = 0
OFF_GR = OFF_XR + D_RNN
OFF_Q = OFF_GR + D_RNN
OFF_K = OFF_Q + Q_W
OFF_V = OFF_K + KV_W
OFF_GA = OFF_V + KV_W
OFF_GB = OFF_GA + D_MODEL
IN_TOTAL = OFF_GB + D_MODEL

V7X_LANES = 128
V7X_SUBLANES = 8
V7X_BF16_SUBLANES = 16
V7X_MXU_DIM = 256
V7X_VMEM_BYTES = 64 * 1024 * 1024

GATE_TILES = D_RNN // V7X_MXU_DIM
SEQ_CHUNK = 256
FFN_ROWS = 512
FF_CHUNK = 1024
V_ROWS = HEAD_DIM + V7X_BF16_SUBLANES
VMEM_LIMIT = V7X_VMEM_BYTES * 3 // 4

BF16 = jnp.bfloat16
F32 = jnp.float32


def _dot(a, b):
    return jnp.dot(a, b, preferred_element_type=F32)


def _dot_nt(a, b):
    return lax.dot_general(a, b, (((1,), (1,)), ((), ())), preferred_element_type=F32)


def _dot_tn(a, b):
    return lax.dot_general(a, b, (((0,), (0,)), ((), ())), preferred_element_type=F32)


def _sigmoid(t):
    return 1.0 / (1.0 + jnp.exp(-t))


def _rmsnorm(t, g):
    ms = jnp.mean(t * t, axis=-1, keepdims=True)
    return t * lax.rsqrt(ms + NORM_EPS) * g


def _head_norm_rope_t(z_t, n_heads, tabs):
    c_lo, s_hi, c_hi, s_lo = tabs
    outs = []
    for hd in range(n_heads):
        t = z_t[hd * HEAD_DIM:(hd + 1) * HEAD_DIM]
        inv = lax.rsqrt(jnp.sum(t * t, axis=0, keepdims=True) * (1.0 / HEAD_DIM) + NORM_EPS)
        t1, t2 = t[:HALF_DIM], t[HALF_DIM:]
        outs.append((t1 * c_lo - t2 * s_hi) * inv)
        outs.append((t2 * c_hi + t1 * s_lo) * inv)
    return jnp.concatenate(outs, axis=0)


def _mixer_kernel(sinks_ref, x_ref, w_in_ref, w_qkv_t_ref, g_mix_ref, conv_w_ref, conv_b_ref, wgate_ref,
                  b_rg_ref, b_ig_ref, lam_ref, w_rnn_ref, rope_ref, bias_ref, w_attn_ref, w_out_ref,
                  o_ref,
                  ext_ref, a_ref, b_ref, hr_ref, hcar_ref, kext_ref, vext_ref, ao_ref):
    ts = SEQ_CHUNK
    j = pl.program_id(1)

    @pl.when(j == 0)
    def _():
        ext_ref[0:V7X_SUBLANES, :] = jnp.zeros((V7X_SUBLANES, D_RNN), F32)
        hcar_ref[...] = jnp.zeros_like(hcar_ref)
        kext_ref[0:WINDOW, :] = jnp.zeros((WINDOW, KV_W), BF16)
        vext_ref[:, 0:HEAD_DIM, 0:WINDOW] = jnp.zeros((N_KV_HEADS, HEAD_DIM, WINDOW), BF16)
        vext_ref[:, HEAD_DIM:V_ROWS, :] = jnp.ones((N_KV_HEADS, V_ROWS - HEAD_DIM, WINDOW + ts), BF16)

    x = x_ref[0]
    h = _rmsnorm(x, g_mix_ref[...]).astype(BF16)

    def proj(lo, width):
        return _dot(h, w_in_ref[:, lo:lo + width])

    xr = proj(OFF_XR, D_RNN)
    ext_ref[V7X_SUBLANES:V7X_SUBLANES + ts, :] = xr
    cw = conv_w_ref[...]
    xc = conv_b_ref[...] + xr * cw[CONV_W - 1:CONV_W]
    for jj in range(CONV_W - 1):
        start = V7X_SUBLANES - (CONV_W - 1) + jj
        xc = xc + ext_ref[start:start + ts, :] * cw[jj:jj + 1]
    ext_ref[0:V7X_SUBLANES, :] = xr[ts - V7X_SUBLANES:ts]

    lam = lam_ref[...]
    neg_c_sp = -LRU_C * (jnp.maximum(-lam, 0.0) + jnp.log1p(jnp.exp(-jnp.abs(lam))))
    for c in range(GATE_TILES):
        sl = slice(c * V7X_MXU_DIM, (c + 1) * V7X_MXU_DIM)
        xcc = xc[:, sl]
        gz = _dot(xcc.astype(BF16), wgate_ref[c])
        r = _sigmoid(gz[:, :V7X_MXU_DIM] + b_rg_ref[:, sl])
        i = _sigmoid(gz[:, V7X_MXU_DIM:] + b_ig_ref[:, sl])
        log_a = neg_c_sp[:, sl] * r
        a = jnp.exp(log_a)
        a_ref[:, sl] = a
        b_ref[:, sl] = jnp.sqrt(-jnp.tanh(log_a) * (a * a + 1.0)) * (i * xcc)

    def scan_body(blk, hc):
        base = pl.multiple_of(blk * V7X_SUBLANES, V7X_SUBLANES)
        for rr in range(V7X_SUBLANES):
            hc = a_ref[pl.ds(base + rr, 1), :] * hc + b_ref[pl.ds(base + rr, 1), :]
            hr_ref[pl.ds(base + rr, 1), :] = hc
        return hc

    hcar_ref[...] = lax.fori_loop(0, ts // V7X_SUBLANES, scan_body, hcar_ref[...])

    gr = proj(OFF_GR, D_RNN)
    gelu = gr * (0.5 * (1.0 + jnp.tanh(math.sqrt(2.0 / math.pi) * (gr + 0.044715 * (gr * gr * gr)))))
    y_a = _dot((hr_ref[...] * gelu).astype(BF16), w_rnn_ref[...])

    qkv_t = _dot_nt(w_qkv_t_ref[...], h)
    q_t = _head_norm_rope_t(qkv_t[0:Q_W], N_Q_HEADS,
                            [rope_ref[t] for t in range(4)]).astype(BF16)
    k_t = _head_norm_rope_t(qkv_t[Q_W:Q_W + KV_W], N_KV_HEADS, [rope_ref[4 + t] for t in range(4)])
    kext_ref[WINDOW:WINDOW + ts, :] = k_t.T.astype(BF16)
    v_t = qkv_t[Q_W + KV_W:QKV_W].astype(BF16)
    for g in range(N_KV_HEADS):
        vext_ref[g, 0:HEAD_DIM, WINDOW:WINDOW + ts] = v_t[g * HEAD_DIM:(g + 1) * HEAD_DIM]

    for n in range(ts // WINDOW):
        cols = slice(n * WINDOW, (n + 1) * WINDOW)
        band = slice(n * WINDOW, (n + 2) * WINDOW)
        bias = bias_ref[1] if n > 0 else bias_ref[jnp.minimum(j, 1)]
        for g in range(N_KV_HEADS):
            heads = [g * GQA_GROUP + i for i in range(GQA_GROUP)]
            q_g = jnp.concatenate([q_t[hd * HEAD_DIM:(hd + 1) * HEAD_DIM, cols] for hd in heads], axis=1)
            s_t = _dot(kext_ref[band, g * HEAD_DIM:(g + 1) * HEAD_DIM], q_g)
            probs, ms = [], []
            for i, hd in enumerate(heads):
                s_i = s_t[:, i * WINDOW:(i + 1) * WINDOW] + bias
                sink = sinks_ref[hd]
                m = jnp.maximum(jnp.max(s_i, axis=0, keepdims=True), sink)
                ms.append(m)
                probs.append(jnp.exp(s_i - m).astype(BF16))
            o_t = _dot(vext_ref[g, :, band], jnp.concatenate(probs, axis=1))
            for i, hd in enumerate(heads):
                blk = slice(i * WINDOW, (i + 1) * WINDOW)
                den = o_t[HEAD_DIM:HEAD_DIM + 1, blk] + jnp.exp(sinks_ref[hd] - ms[i])
                ao_ref[hd * HEAD_DIM:(hd + 1) * HEAD_DIM, cols] = o_t[0:HEAD_DIM, blk] / den
    kext_ref[0:WINDOW, :] = kext_ref[ts:ts + WINDOW, :]
    vext_ref[:, 0:HEAD_DIM, 0:WINDOW] = vext_ref[:, 0:HEAD_DIM, ts:ts + WINDOW]
    y_b = _dot_tn(ao_ref[...].astype(BF16), w_attn_ref[...])

    g_a = _sigmoid(proj(OFF_GA, D_MODEL))
    g_b = _sigmoid(proj(OFF_GB, D_MODEL))
    merged = (g_a * y_a + g_b * y_b).astype(BF16)
    o_ref[0] = x + _dot(merged, w_out_ref[...])


def _ffn_kernel(x_ref, p_ref, g_mlp_ref, w_up_ref, w_down_ref, g_ple_ref, w_pg_ref, w_pp_ref, o_ref):
    x = x_ref[...]
    hm = _rmsnorm(x, g_mlp_ref[...]).astype(BF16)
    acc = x
    for c in range(D_FF // FF_CHUNK):
        u = jnp.maximum(_dot(hm, w_up_ref[:, c * FF_CHUNK:(c + 1) * FF_CHUNK]), 0.0)
        acc = acc + _dot((u * u).astype(BF16), w_down_ref[c * FF_CHUNK:(c + 1) * FF_CHUNK, :])
    hp = _rmsnorm(acc, g_ple_ref[...]).astype(BF16)
    gate = _sigmoid(_dot(hp, w_pg_ref[...]))
    e = _dot(p_ref[...].astype(BF16), w_pp_ref[...])
    o_ref[...] = acc + e * gate


def _resident(shape):
    nd = len(shape)
    return pl.BlockSpec(shape, lambda *_: (0,) * nd, pipeline_mode=pl.Buffered(1))


def _block_diag_tiles(w):
    per = V7X_MXU_DIM // RNN_BLOCK_W
    w4 = w.reshape(GATE_TILES, per, RNN_BLOCK_W, RNN_BLOCK_W)
    eye = jnp.eye(per, dtype=w.dtype)
    return jnp.einsum('cgij,gh->cgihj', w4, eye).reshape(GATE_TILES, V7X_MXU_DIM, V7X_MXU_DIM)


def _rope_gain_tables(seq, gain, scale):
    inv = ROPE_THETA ** (-jnp.arange(0, HEAD_DIM, 2, dtype=F32) / HEAD_DIM)
    ang = inv[:, None] * jnp.arange(seq, dtype=F32)[None, :]
    cos, sin = jnp.cos(ang), jnp.sin(ang)
    g_lo = (gain[:HALF_DIM] * scale)[:, None]
    g_hi = (gain[HALF_DIM:] * scale)[:, None]
    return [g_lo * cos, g_hi * sin, g_hi * cos, g_lo * sin]


def _band_bias_t():
    qi = jnp.arange(WINDOW)[None, :]
    ci = jnp.arange(2 * WINDOW)[:, None]
    diff = WINDOW + qi - ci
    valid = (diff >= 0) & (diff < WINDOW)
    first = valid & (ci >= WINDOW)
    neg = jnp.float32(-jnp.inf)
    return jnp.stack([jnp.where(first, 0.0, neg), jnp.where(valid, 0.0, neg)]).astype(F32)


def _mixer(x, sinks, mats, vecs):
    batch, seq, _ = x.shape
    ts = SEQ_CHUNK
    assert seq % ts == 0 and ts % WINDOW == 0
    (w_in, w_qkv_t, wgate, w_rnn, w_attn, w_out) = mats
    (g_mix, conv_w, conv_b, b_rg, b_ig, lam, rope, bias) = vecs
    in_specs = [
        pl.BlockSpec((1, ts, D_MODEL), lambda b, j, *_: (b, j, 0)),
        _resident(w_in.shape), _resident(w_qkv_t.shape), _resident(g_mix.shape), _resident(conv_w.shape),
        _resident(conv_b.shape), _resident(wgate.shape), _resident(b_rg.shape), _resident(b_ig.shape),
        _resident(lam.shape), _resident(w_rnn.shape),
        pl.BlockSpec((rope.shape[0], HALF_DIM, ts), lambda b, j, *_: (0, 0, j)),
        _resident(bias.shape), _resident(w_attn.shape), _resident(w_out.shape),
    ]
    grid_spec = pltpu.PrefetchScalarGridSpec(
        num_scalar_prefetch=1,
        grid=(batch, seq // ts),
        in_specs=in_specs,
        out_specs=pl.BlockSpec((1, ts, D_MODEL), lambda b, j, *_: (b, j, 0)),
        scratch_shapes=[
            pltpu.VMEM((ts + V7X_SUBLANES, D_RNN), F32),
            pltpu.VMEM((ts, D_RNN), F32),
            pltpu.VMEM((ts, D_RNN), F32),
            pltpu.VMEM((ts, D_RNN), F32),
            pltpu.VMEM((1, D_RNN), F32),
            pltpu.VMEM((WINDOW + ts, KV_W), BF16),
            pltpu.VMEM((N_KV_HEADS, V_ROWS, WINDOW + ts), BF16),
            pltpu.VMEM((Q_W, ts), F32),
        ],
    )
    return pl.pallas_call(
        _mixer_kernel,
        grid_spec=grid_spec,
        out_shape=jax.ShapeDtypeStruct(x.shape, F32),
        compiler_params=pltpu.CompilerParams(
            dimension_semantics=("arbitrary", "arbitrary"), vmem_limit_bytes=VMEM_LIMIT),
        name="mixer",
    )(sinks, x, w_in, w_qkv_t, g_mix, conv_w, conv_b, wgate, b_rg, b_ig, lam, w_rnn, rope, bias,
      w_attn, w_out)


def _ffn(x2d, p2d, g_mlp, w_up, w_down, g_ple, w_pg, w_pp):
    rows = x2d.shape[0]
    tm = FFN_ROWS
    assert rows % tm == 0
    return pl.pallas_call(
        _ffn_kernel,
        grid=(rows // tm,),
        in_specs=[
            pl.BlockSpec((tm, D_MODEL), lambda i: (i, 0)),
            pl.BlockSpec((tm, PLE_DIM), lambda i: (i, 0)),
            _resident(g_mlp.shape), _resident(w_up.shape), _resident(w_down.shape),
            _resident(g_ple.shape), _resident(w_pg.shape), _resident(w_pp.shape),
        ],
        out_specs=pl.BlockSpec((tm, D_MODEL), lambda i: (i, 0)),
        out_shape=jax.ShapeDtypeStruct(x2d.shape, F32),
        compiler_params=pltpu.CompilerParams(
            dimension_semantics=("arbitrary",), vmem_limit_bytes=VMEM_LIMIT),
        name="ffn",
    )(x2d, p2d, g_mlp, w_up, w_down, g_ple, w_pg, w_pp)


def kernel(x, p, g_mix, w_in, conv_w, conv_b, w_rg, b_rg, w_ig, b_ig, lru_lambda, w_rnn_proj, q_gain, k_gain, sinks, w_attn_proj, w_out, g_mlp, w_up, w_down, g_ple, w_ple_gate, w_ple_proj):
    batch, seq, _ = x.shape
    depth = p.shape[0]
    bias = _band_bias_t()
    row = lambda v: v.reshape(1, -1)
    for l in range(depth):
        rope = jnp.stack(_rope_gain_tables(seq, q_gain[l], HEAD_DIM ** -0.5)
                         + _rope_gain_tables(seq, k_gain[l], 1.0))
        wgate = jnp.concatenate([_block_diag_tiles(w_rg[l]), _block_diag_tiles(w_ig[l])], axis=2).astype(BF16)
        w_in_b = w_in[l].astype(BF16)
        mats = (w_in_b, w_in_b[:, OFF_Q:OFF_GA].T, wgate, w_rnn_proj[l].astype(BF16),
                w_attn_proj[l].astype(BF16), w_out[l].astype(BF16))
        vecs = (row(g_mix[l]), conv_w[l], row(conv_b[l]), row(b_rg[l]), row(b_ig[l]), row(lru_lambda[l]),
                rope, bias)
        x = _mixer(x, sinks[l], mats, vecs)
        x = _ffn(x.reshape(batch * seq, D_MODEL), p[l].reshape(batch * seq, PLE_DIM), row(g_mlp[l]),
                 w_up[l].astype(BF16), w_down[l].astype(BF16), row(g_ple[l]),
                 w_ple_gate[l].astype(BF16), w_ple_proj[l].astype(BF16)).reshape(batch, seq, D_MODEL)
    return x
```

```python
import math

import jax
import jax.numpy as jnp
from jax import lax
from jax.experimental import pallas as pl
from jax.experimental.pallas import tpu as pltpu

D_MODEL = 1024
D_RNN = D_MODEL
RNN_BLOCKS = 16
RNN_BLOCK_W = D_RNN // RNN_BLOCKS
CONV_W = 4
LRU_C = 8.0
HEAD_DIM = 64
HALF_DIM = HEAD_DIM // 2
N_Q_HEADS = D_MODEL // HEAD_DIM
N_KV_HEADS = 4
GQA_GROUP = N_Q_HEADS // N_KV_HEADS
WINDOW = 128
ROPE_THETA = 10000.0
Q_W = N_Q_HEADS * HEAD_DIM
KV_W = N_KV_HEADS * HEAD_DIM
QKV_W = Q_W + 2 * KV_W
D_FF = 4 * D_MODEL
PLE_DIM = 256
NORM_EPS = 1e-6
LOG2E = math.log2(math.e)
GELU_C0 = math.sqrt(2.0 / math.pi)
GELU_C1 = GELU_C0 * 0.044715

REF_OFF_XR = 0
REF_OFF_GR = REF_OFF_XR + D_RNN
REF_OFF_Q = REF_OFF_GR + D_RNN
REF_OFF_GA = REF_OFF_Q + QKV_W
REF_OFF_GB = REF_OFF_GA + D_MODEL
OFF_XR, OFF_GR, OFF_GA, OFF_GB = 0, D_RNN, 2 * D_RNN, 2 * D_RNN + D_MODEL

V7X_LANES = 128
V7X_SUBLANES = 8
V7X_BF16_SUBLANES = 16
V7X_MXU_DIM = 256
V7X_VMEM_BYTES = 64 * 1024 * 1024

GATE_TILES = D_RNN // V7X_MXU_DIM
SEQ_CHUNK = 256
FFN_ROWS = 512
FF_CHUNK = 1024
V_ROWS = HEAD_DIM + V7X_BF16_SUBLANES
VMEM_LIMIT = V7X_VMEM_BYTES * 3 // 4

BF16 = jnp.bfloat16
F32 = jnp.float32


def _dot(a, b):
    return jnp.dot(a, b, preferred_element_type=F32)


def _dot_tn(a, b):
    return lax.dot_general(a, b, (((0,), (0,)), ((), ())), preferred_element_type=F32)


def _sigmoid(t):
    return 1.0 / (1.0 + jnp.exp(-t))


def _rmsnorm(t, g):
    ms = jnp.mean(t * t, axis=-1, keepdims=True)
    return t * lax.rsqrt(ms + NORM_EPS) * g


def _head_norm_rope_t(z_t, n_heads, tabs):
    c_lo, s_hi, c_hi, s_lo = tabs
    outs = []
    for hd in range(n_heads):
        t = z_t[hd * HEAD_DIM:(hd + 1) * HEAD_DIM]
        inv = lax.rsqrt(jnp.sum(t * t, axis=0, keepdims=True) * (1.0 / HEAD_DIM) + NORM_EPS)
        t1, t2 = t[:HALF_DIM], t[HALF_DIM:]
        outs.append((t1 * c_lo - t2 * s_hi) * inv)
        outs.append((t2 * c_hi + t1 * s_lo) * inv)
    return jnp.concatenate(outs, axis=0)


def _mixer_kernel(sinks2_ref, x_ref, w_tok_ref, w_qkv_t_ref, g_mix_ref, conv_w_ref, conv_b_ref, wgate_ref,
                  hb_rg_ref, hb_ig_ref, lam_ref, w_rnn_ref, rope_ref, bias_ref, w_attn_ref, w_out_ref,
                  o_ref,
                  ext_ref, a_ref, b_ref, hr_ref, hcar_ref, kext_ref, vext_ref, ao_ref):
    ts = SEQ_CHUNK
    j = pl.program_id(1)

    @pl.when(j == 0)
    def _():
        ext_ref[0:V7X_SUBLANES, :] = jnp.zeros((V7X_SUBLANES, D_RNN), F32)
        hcar_ref[...] = jnp.zeros_like(hcar_ref)
        kext_ref[0:WINDOW, :] = jnp.zeros((WINDOW, KV_W), BF16)
        vext_ref[:, 0:HEAD_DIM, 0:WINDOW] = jnp.zeros((N_KV_HEADS, HEAD_DIM, WINDOW), BF16)
        vext_ref[:, HEAD_DIM:V_ROWS, :] = jnp.ones((N_KV_HEADS, V_ROWS - HEAD_DIM, WINDOW + ts), BF16)

    x = x_ref[0]
    hn = _rmsnorm(x, g_mix_ref[...])
    h = hn.astype(BF16)
    h_t = hn.T.astype(BF16)

    def proj(lo, width):
        return _dot(h, w_tok_ref[:, lo:lo + width])

    xr = proj(OFF_XR, D_RNN)
    ext_ref[V7X_SUBLANES:V7X_SUBLANES + ts, :] = xr
    cw = conv_w_ref[...]
    xc = conv_b_ref[...] + xr * cw[CONV_W - 1:CONV_W]
    for jj in range(CONV_W - 1):
        start = V7X_SUBLANES - (CONV_W - 1) + jj
        xc = xc + ext_ref[start:start + ts, :] * cw[jj:jj + 1]
    ext_ref[0:V7X_SUBLANES, :] = xr[ts - V7X_SUBLANES:ts]

    lam = lam_ref[...]
    half_c_sp = (0.5 * LRU_C) * (jnp.maximum(-lam, 0.0) + jnp.log1p(jnp.exp(-jnp.abs(lam))))

    qkv_rows = QKV_W // GATE_TILES
    ya_parts, ta_parts, tb_parts, qkv_parts = [], [], [], []
    for c in range(GATE_TILES):
        sl = slice(c * V7X_MXU_DIM, (c + 1) * V7X_MXU_DIM)
        xcc = xc[:, sl]
        gz = _dot(xcc.astype(BF16), wgate_ref[c])
        t_r = jnp.tanh(gz[:, :V7X_MXU_DIM] + hb_rg_ref[:, sl])
        t_i = jnp.tanh(gz[:, V7X_MXU_DIM:] + hb_ig_ref[:, sl])
        hcs = half_c_sp[:, sl]
        neg_log_a = hcs + hcs * t_r
        a = jnp.exp2(neg_log_a * (-LOG2E))
        a_ref[:, sl] = a
        y = jnp.tanh(neg_log_a) * (a * a + 1.0)
        xh = 0.5 * xcc
        b_ref[:, sl] = jnp.where(y > 0.0, y * lax.rsqrt(y), 0.0) * (xh + xh * t_i)

        gr = proj(OFF_GR + c * V7X_MXU_DIM, V7X_MXU_DIM)
        ya_parts.append((gr, jnp.tanh(gr * (GELU_C0 + GELU_C1 * (gr * gr)))))
        ta_parts.append(jnp.tanh(proj(OFF_GA + c * V7X_MXU_DIM, V7X_MXU_DIM)))
        tb_parts.append(jnp.tanh(proj(OFF_GB + c * V7X_MXU_DIM, V7X_MXU_DIM)))
        qkv_parts.append(_dot(w_qkv_t_ref[c * qkv_rows:(c + 1) * qkv_rows, :], h_t))
    qkv_t = jnp.concatenate(qkv_parts, axis=0)

    q_t = _head_norm_rope_t(qkv_t[0:Q_W], N_Q_HEADS,
                            [rope_ref[t] for t in range(4)]).astype(BF16)
    k_t = _head_norm_rope_t(qkv_t[Q_W:Q_W + KV_W], N_KV_HEADS, [rope_ref[4 + t] for t in range(4)])
    kext_ref[WINDOW:WINDOW + ts, :] = k_t.T.astype(BF16)
    v_t = qkv_t[Q_W + KV_W:QKV_W].astype(BF16)
    for g in range(N_KV_HEADS):
        vext_ref[g, 0:HEAD_DIM, WINDOW:WINDOW + ts] = v_t[g * HEAD_DIM:(g + 1) * HEAD_DIM]

    hc = hcar_ref[...]
    for t in range(ts):
        hc = a_ref[t:t + 1, :] * hc + b_ref[t:t + 1, :]
        hr_ref[t:t + 1, :] = hc
    hcar_ref[...] = hc
    ya_in = []
    for c in range(GATE_TILES):
        gr, tg = ya_parts[c]
        u = hr_ref[:, c * V7X_MXU_DIM:(c + 1) * V7X_MXU_DIM] * gr
        ya_in.append((u + u * tg).astype(BF16))
    y_a = _dot(jnp.concatenate(ya_in, axis=1), w_rnn_ref[...])

    def scores(n, g):
        cols = slice(n * WINDOW, (n + 1) * WINDOW)
        q_g = jnp.concatenate([q_t[hd * HEAD_DIM:(hd + 1) * HEAD_DIM, cols]
                               for hd in range(g * GQA_GROUP, (g + 1) * GQA_GROUP)], axis=1)
        return _dot(kext_ref[n * WINDOW:(n + 2) * WINDOW, g * HEAD_DIM:(g + 1) * HEAD_DIM], q_g)

    def attend(n, g, s_t):
        cols = slice(n * WINDOW, (n + 1) * WINDOW)
        bias = bias_ref[1] if n > 0 else bias_ref[jnp.minimum(j, 1)]
        probs, ms = [], []
        for i in range(GQA_GROUP):
            s_i = s_t[:, i * WINDOW:(i + 1) * WINDOW] + bias
            m = jnp.maximum(jnp.max(s_i, axis=0, keepdims=True), sinks2_ref[g * GQA_GROUP + i])
            ms.append(m)
            probs.append(jnp.exp2(s_i - m).astype(BF16))
        o_t = _dot(vext_ref[g, :, n * WINDOW:(n + 2) * WINDOW], jnp.concatenate(probs, axis=1))
        for i in range(GQA_GROUP):
            hd = g * GQA_GROUP + i
            blk = slice(i * WINDOW, (i + 1) * WINDOW)
            den = o_t[HEAD_DIM:HEAD_DIM + 1, blk] + jnp.exp2(sinks2_ref[hd] - ms[i])
            ao_ref[hd * HEAD_DIM:(hd + 1) * HEAD_DIM, cols] = o_t[0:HEAD_DIM, blk] / den

    work = [(n, g) for n in range(ts // WINDOW) for g in range(N_KV_HEADS)]
    s_next = scores(*work[0])
    for idx, (n, g) in enumerate(work):
        s_cur = s_next
        if idx + 1 < len(work):
            s_next = scores(*work[idx + 1])
        attend(n, g, s_cur)
    kext_ref[0:WINDOW, :] = kext_ref[ts:ts + WINDOW, :]
    vext_ref[:, 0:HEAD_DIM, 0:WINDOW] = vext_ref[:, 0:HEAD_DIM, ts:ts + WINDOW]
    y_b = _dot_tn(ao_ref[...].astype(BF16), w_attn_ref[...])

    t_a = jnp.concatenate(ta_parts, axis=1)
    t_b = jnp.concatenate(tb_parts, axis=1)
    merged = ((y_a + t_a * y_a) + (y_b + t_b * y_b)).astype(BF16)
    o_ref[0] = x + _dot(merged, w_out_ref[...])


def _ffn_kernel(x_ref, p_ref, g_mlp_ref, w_up_ref, w_down_ref, g_ple_ref, w_pg_ref, w_pp_ref, o_ref):
    x = x_ref[...]
    hm = _rmsnorm(x, g_mlp_ref[...]).astype(BF16)
    acc = x
    for c in range(D_FF // FF_CHUNK):
        u = jnp.maximum(_dot(hm, w_up_ref[:, c * FF_CHUNK:(c + 1) * FF_CHUNK]), 0.0)
        acc = acc + _dot((u * u).astype(BF16), w_down_ref[c * FF_CHUNK:(c + 1) * FF_CHUNK, :])
    hp = _rmsnorm(acc, g_ple_ref[...]).astype(BF16)
    gate = _sigmoid(_dot(hp, w_pg_ref[...]))
    e = _dot(p_ref[...].astype(BF16), w_pp_ref[...])
    o_ref[...] = acc + e * gate


def _resident(shape):
    nd = len(shape)
    return pl.BlockSpec(shape, lambda *_: (0,) * nd, pipeline_mode=pl.Buffered(1))


def _block_diag_tiles(w):
    per = V7X_MXU_DIM // RNN_BLOCK_W
    w4 = w.reshape(GATE_TILES, per, RNN_BLOCK_W, RNN_BLOCK_W)
    eye = jnp.eye(per, dtype=w.dtype)
    return jnp.einsum('cgij,gh->cgihj', w4, eye).reshape(GATE_TILES, V7X_MXU_DIM, V7X_MXU_DIM)


def _rope_gain_tables(seq, gain, scale):
    inv = ROPE_THETA ** (-jnp.arange(0, HEAD_DIM, 2, dtype=F32) / HEAD_DIM)
    ang = inv[:, None] * jnp.arange(seq, dtype=F32)[None, :]
    cos, sin = jnp.cos(ang), jnp.sin(ang)
    g_lo = (gain[:HALF_DIM] * scale)[:, None]
    g_hi = (gain[HALF_DIM:] * scale)[:, None]
    return [g_lo * cos, g_hi * sin, g_hi * cos, g_lo * sin]


def _band_bias_t():
    qi = jnp.arange(WINDOW)[None, :]
    ci = jnp.arange(2 * WINDOW)[:, None]
    diff = WINDOW + qi - ci
    valid = (diff >= 0) & (diff < WINDOW)
    first = valid & (ci >= WINDOW)
    neg = jnp.float32(-jnp.inf)
    return jnp.stack([jnp.where(first, 0.0, neg), jnp.where(valid, 0.0, neg)]).astype(F32)


def _mixer(x, sinks2, mats, vecs):
    batch, seq, _ = x.shape
    ts = SEQ_CHUNK
    assert seq % ts == 0 and ts % WINDOW == 0
    (w_tok, w_qkv_t, wgate, w_rnn, w_attn, w_out) = mats
    (g_mix, conv_w, conv_b, hb_rg, hb_ig, lam, rope, bias) = vecs
    in_specs = [
        pl.BlockSpec((1, ts, D_MODEL), lambda b, j, *_: (b, j, 0)),
        _resident(w_tok.shape), _resident(w_qkv_t.shape), _resident(g_mix.shape), _resident(conv_w.shape),
        _resident(conv_b.shape), _resident(wgate.shape), _resident(hb_rg.shape), _resident(hb_ig.shape),
        _resident(lam.shape), _resident(w_rnn.shape),
        pl.BlockSpec((rope.shape[0], HALF_DIM, ts), lambda b, j, *_: (0, 0, j)),
        _resident(bias.shape), _resident(w_attn.shape), _resident(w_out.shape),
    ]
    grid_spec = pltpu.PrefetchScalarGridSpec(
        num_scalar_prefetch=1,
        grid=(batch, seq // ts),
        in_specs=in_specs,
        out_specs=pl.BlockSpec((1, ts, D_MODEL), lambda b, j, *_: (b, j, 0)),
        scratch_shapes=[
            pltpu.VMEM((ts + V7X_SUBLANES, D_RNN), F32),
            pltpu.VMEM((ts, D_RNN), F32),
            pltpu.VMEM((ts, D_RNN), F32),
            pltpu.VMEM((ts, D_RNN), F32),
            pltpu.VMEM((1, D_RNN), F32),
            pltpu.VMEM((WINDOW + ts, KV_W), BF16),
            pltpu.VMEM((N_KV_HEADS, V_ROWS, WINDOW + ts), BF16),
            pltpu.VMEM((Q_W, ts), F32),
        ],
    )
    return pl.pallas_call(
        _mixer_kernel,
        grid_spec=grid_spec,
        out_shape=jax.ShapeDtypeStruct(x.shape, F32),
        compiler_params=pltpu.CompilerParams(
            dimension_semantics=("arbitrary", "arbitrary"), vmem_limit_bytes=VMEM_LIMIT),
        name="mixer",
    )(sinks2, x, w_tok, w_qkv_t, g_mix, conv_w, conv_b, wgate, hb_rg, hb_ig, lam, w_rnn, rope, bias,
      w_attn, w_out)


def _ffn(x2d, p2d, g_mlp, w_up, w_down, g_ple, w_pg, w_pp):
    rows = x2d.shape[0]
    tm = FFN_ROWS
    assert rows % tm == 0
    return pl.pallas_call(
        _ffn_kernel,
        grid=(rows // tm,),
        in_specs=[
            pl.BlockSpec((tm, D_MODEL), lambda i: (i, 0)),
            pl.BlockSpec((tm, PLE_DIM), lambda i: (i, 0)),
            _resident(g_mlp.shape), _resident(w_up.shape), _resident(w_down.shape),
            _resident(g_ple.shape), _resident(w_pg.shape), _resident(w_pp.shape),
        ],
        out_specs=pl.BlockSpec((tm, D_MODEL), lambda i: (i, 0)),
        out_shape=jax.ShapeDtypeStruct(x2d.shape, F32),
        compiler_params=pltpu.CompilerParams(
            dimension_semantics=("arbitrary",), vmem_limit_bytes=VMEM_LIMIT),
        name="ffn",
    )(x2d, p2d, g_mlp, w_up, w_down, g_ple, w_pg, w_pp)


def kernel(x, p, g_mix, w_in, conv_w, conv_b, w_rg, b_rg, w_ig, b_ig, lru_lambda, w_rnn_proj, q_gain, k_gain, sinks, w_attn_proj, w_out, g_mlp, w_up, w_down, g_ple, w_ple_gate, w_ple_proj):
    batch, seq, _ = x.shape
    depth = p.shape[0]
    bias = _band_bias_t()
    row = lambda v: v.reshape(1, -1)
    for l in range(depth):
        rope = jnp.stack(_rope_gain_tables(seq, q_gain[l], LOG2E * HEAD_DIM ** -0.5)
                         + _rope_gain_tables(seq, k_gain[l], 1.0))
        wl = w_in[l]
        w_tok = jnp.concatenate([wl[:, REF_OFF_XR:REF_OFF_Q], 0.5 * wl[:, REF_OFF_GA:]], axis=1).astype(BF16)
        w_qkv_t = wl[:, REF_OFF_Q:REF_OFF_GA].T.astype(BF16)
        wgate = (0.5 * jnp.concatenate([_block_diag_tiles(w_rg[l]), _block_diag_tiles(w_ig[l])],
                                       axis=2)).astype(BF16)
        mats = (w_tok, w_qkv_t, wgate, (0.5 * w_rnn_proj[l]).astype(BF16),
                w_attn_proj[l].astype(BF16), (0.5 * w_out[l]).astype(BF16))
        vecs = (row(g_mix[l]), conv_w[l], row(conv_b[l]), row(0.5 * b_rg[l]), row(0.5 * b_ig[l]),
                row(lru_lambda[l]), rope, bias)
        x = _mixer(x, sinks[l] * LOG2E, mats, vecs)
        x = _ffn(x.reshape(batch * seq, D_MODEL), p[l].reshape(batch * seq, PLE_DIM), row(g_mlp[l]),
                 w_up[l].astype(BF16), w_down[l].astype(BF16), row(g_ple[l]),
                 w_ple_gate[l].astype(BF16), w_ple_proj[l].astype(BF16)).reshape(batch, seq, D_MODEL)
    return x
```

```python
import math

import jax
import jax.numpy as jnp
from jax import lax
from jax.experimental import pallas as pl
from jax.experimental.pallas import tpu as pltpu

D_MODEL = 1024
D_RNN = D_MODEL
RNN_BLOCKS = 16
RNN_BLOCK_W = D_RNN // RNN_BLOCKS
CONV_W = 4
LRU_C = 8.0
HEAD_DIM = 64
HALF_DIM = HEAD_DIM // 2
N_Q_HEADS = D_MODEL // HEAD_DIM
N_KV_HEADS = 4
GQA_GROUP = N_Q_HEADS // N_KV_HEADS
WINDOW = 128
ROPE_THETA = 10000.0
Q_W = N_Q_HEADS * HEAD_DIM
KV_W = N_KV_HEADS * HEAD_DIM
QKV_W = Q_W + 2 * KV_W
D_FF = 4 * D_MODEL
PLE_DIM = 256
NORM_EPS = 1e-6
LOG2E = math.log2(math.e)
GELU_C0 = math.sqrt(2.0 / math.pi)
GELU_C1 = GELU_C0 * 0.044715

REF_OFF_XR = 0
REF_OFF_GR = REF_OFF_XR + D_RNN
REF_OFF_Q = REF_OFF_GR + D_RNN
REF_OFF_GA = REF_OFF_Q + QKV_W
REF_OFF_GB = REF_OFF_GA + D_MODEL
OFF_XR, OFF_GR, OFF_GA, OFF_GB = 0, D_RNN, 2 * D_RNN, 2 * D_RNN + D_MODEL

V7X_LANES = 128
V7X_SUBLANES = 8
V7X_BF16_SUBLANES = 16
V7X_MXU_DIM = 256
V7X_VMEM_BYTES = 64 * 1024 * 1024

GATE_TILES = D_RNN // V7X_MXU_DIM
SEQ_CHUNK = 512
FFN_ROWS = 512
FF_CHUNK = 1024
V_ROWS = HEAD_DIM + V7X_BF16_SUBLANES
VMEM_LIMIT = V7X_VMEM_BYTES * 3 // 4

BF16 = jnp.bfloat16
F32 = jnp.float32


def _dot(a, b):
    return jnp.dot(a, b, preferred_element_type=F32)


def _dot_tn(a, b):
    return lax.dot_general(a, b, (((0,), (0,)), ((), ())), preferred_element_type=F32)


def _sigmoid(t):
    return 1.0 / (1.0 + jnp.exp(-t))


def _rmsnorm(t, g):
    ms = jnp.mean(t * t, axis=-1, keepdims=True)
    return t * lax.rsqrt(ms + NORM_EPS) * g


def _head_norm_rope_t(z_t, n_heads, tabs):
    c_lo, s_hi, c_hi, s_lo = tabs
    outs = []
    for hd in range(n_heads):
        t = z_t[hd * HEAD_DIM:(hd + 1) * HEAD_DIM]
        inv = lax.rsqrt(jnp.sum(t * t, axis=0, keepdims=True) * (1.0 / HEAD_DIM) + NORM_EPS)
        t1, t2 = t[:HALF_DIM], t[HALF_DIM:]
        outs.append((t1 * c_lo - t2 * s_hi) * inv)
        outs.append((t2 * c_hi + t1 * s_lo) * inv)
    return jnp.concatenate(outs, axis=0)


def _mixer_kernel(sinks2_ref, x_ref, w_tok_ref, w_qkv_t_ref, g_mix_ref, conv_w_ref, conv_b_ref, wgate_ref,
                  hb_rg_ref, hb_ig_ref, lam_ref, w_rnn_ref, rope_ref, bias_ref, w_attn_ref, w_out_ref,
                  o_ref,
                  ext_ref, a_ref, b_ref, hr_ref, hcar_ref, kext_ref, vext_ref, ao_ref):
    ts = SEQ_CHUNK
    j = pl.program_id(1)

    @pl.when(j == 0)
    def _():
        ext_ref[0:V7X_SUBLANES, :] = jnp.zeros((V7X_SUBLANES, D_RNN), F32)
        hcar_ref[...] = jnp.zeros_like(hcar_ref)
        kext_ref[0:WINDOW, :] = jnp.zeros((WINDOW, KV_W), BF16)
        vext_ref[:, 0:HEAD_DIM, 0:WINDOW] = jnp.zeros((N_KV_HEADS, HEAD_DIM, WINDOW), BF16)
        vext_ref[:, HEAD_DIM:V_ROWS, :] = jnp.ones((N_KV_HEADS, V_ROWS - HEAD_DIM, WINDOW + ts), BF16)

    x = x_ref[0]
    hn = _rmsnorm(x, g_mix_ref[...])
    h = hn.astype(BF16)
    h_t = hn.T.astype(BF16)

    def proj(lo, width):
        return _dot(h, w_tok_ref[:, lo:lo + width])

    xr = proj(OFF_XR, D_RNN)
    ext_ref[V7X_SUBLANES:V7X_SUBLANES + ts, :] = xr
    cw = conv_w_ref[...]
    xc = conv_b_ref[...] + xr * cw[CONV_W - 1:CONV_W]
    for jj in range(CONV_W - 1):
        start = V7X_SUBLANES - (CONV_W - 1) + jj
        xc = xc + ext_ref[start:start + ts, :] * cw[jj:jj + 1]
    ext_ref[0:V7X_SUBLANES, :] = xr[ts - V7X_SUBLANES:ts]

    lam = lam_ref[...]
    half_c_sp = (0.5 * LRU_C) * (jnp.maximum(-lam, 0.0) + jnp.log1p(jnp.exp(-jnp.abs(lam))))

    qkv_rows = QKV_W // GATE_TILES
    ya_parts, ta_parts, tb_parts, qkv_parts = [], [], [], []
    for c in range(GATE_TILES):
        sl = slice(c * V7X_MXU_DIM, (c + 1) * V7X_MXU_DIM)
        xcc = xc[:, sl]
        gz = _dot(xcc.astype(BF16), wgate_ref[c])
        t_r = jnp.tanh(gz[:, :V7X_MXU_DIM] + hb_rg_ref[:, sl])
        t_i = jnp.tanh(gz[:, V7X_MXU_DIM:] + hb_ig_ref[:, sl])
        hcs = half_c_sp[:, sl]
        neg_log_a = hcs + hcs * t_r
        a = jnp.exp2(neg_log_a * (-LOG2E))
        a_ref[:, sl] = a
        y = jnp.tanh(neg_log_a) * (a * a + 1.0)
        xh = 0.5 * xcc
        b_ref[:, sl] = jnp.where(y > 0.0, y * lax.rsqrt(y), 0.0) * (xh + xh * t_i)

        gr = proj(OFF_GR + c * V7X_MXU_DIM, V7X_MXU_DIM)
        ya_parts.append((gr, jnp.tanh(gr * (GELU_C0 + GELU_C1 * (gr * gr)))))
        ta_parts.append(jnp.tanh(proj(OFF_GA + c * V7X_MXU_DIM, V7X_MXU_DIM)))
        tb_parts.append(jnp.tanh(proj(OFF_GB + c * V7X_MXU_DIM, V7X_MXU_DIM)))
        qkv_parts.append(_dot(w_qkv_t_ref[c * qkv_rows:(c + 1) * qkv_rows, :], h_t))
    qkv_t = jnp.concatenate(qkv_parts, axis=0)

    q_t = _head_norm_rope_t(qkv_t[0:Q_W], N_Q_HEADS,
                            [rope_ref[t] for t in range(4)]).astype(BF16)
    k_t = _head_norm_rope_t(qkv_t[Q_W:Q_W + KV_W], N_KV_HEADS, [rope_ref[4 + t] for t in range(4)])
    kext_ref[WINDOW:WINDOW + ts, :] = k_t.T.astype(BF16)
    v_t = qkv_t[Q_W + KV_W:QKV_W].astype(BF16)
    for g in range(N_KV_HEADS):
        vext_ref[g, 0:HEAD_DIM, WINDOW:WINDOW + ts] = v_t[g * HEAD_DIM:(g + 1) * HEAD_DIM]

    hc = hcar_ref[...]
    for t in range(ts):
        hc = a_ref[t:t + 1, :] * hc + b_ref[t:t + 1, :]
        hr_ref[t:t + 1, :] = hc
    hcar_ref[...] = hc
    ya_in = []
    for c in range(GATE_TILES):
        gr, tg = ya_parts[c]
        u = hr_ref[:, c * V7X_MXU_DIM:(c + 1) * V7X_MXU_DIM] * gr
        ya_in.append((u + u * tg).astype(BF16))
    y_a = _dot(jnp.concatenate(ya_in, axis=1), w_rnn_ref[...])

    def scores(n, g):
        cols = slice(n * WINDOW, (n + 1) * WINDOW)
        q_g = jnp.concatenate([q_t[hd * HEAD_DIM:(hd + 1) * HEAD_DIM, cols]
                               for hd in range(g * GQA_GROUP, (g + 1) * GQA_GROUP)], axis=1)
        return _dot(kext_ref[n * WINDOW:(n + 2) * WINDOW, g * HEAD_DIM:(g + 1) * HEAD_DIM], q_g)

    def attend(n, g, s_t):
        cols = slice(n * WINDOW, (n + 1) * WINDOW)
        bias = bias_ref[1] if n > 0 else bias_ref[jnp.minimum(j, 1)]
        probs, ms = [], []
        for i in range(GQA_GROUP):
            s_i = s_t[:, i * WINDOW:(i + 1) * WINDOW] + bias
            m = jnp.maximum(jnp.max(s_i, axis=0, keepdims=True), sinks2_ref[g * GQA_GROUP + i])
            ms.append(m)
            probs.append(jnp.exp2(s_i - m).astype(BF16))
        o_t = _dot(vext_ref[g, :, n * WINDOW:(n + 2) * WINDOW], jnp.concatenate(probs, axis=1))
        for i in range(GQA_GROUP):
            hd = g * GQA_GROUP + i
            blk = slice(i * WINDOW, (i + 1) * WINDOW)
            den = o_t[HEAD_DIM:HEAD_DIM + 1, blk] + jnp.exp2(sinks2_ref[hd] - ms[i])
            ao_ref[hd * HEAD_DIM:(hd + 1) * HEAD_DIM, cols] = o_t[0:HEAD_DIM, blk] / den

    work = [(n, g) for n in range(ts // WINDOW) for g in range(N_KV_HEADS)]
    s_next = scores(*work[0])
    for idx, (n, g) in enumerate(work):
        s_cur = s_next
        if idx + 1 < len(work):
            s_next = scores(*work[idx + 1])
        attend(n, g, s_cur)
    kext_ref[0:WINDOW, :] = kext_ref[ts:ts + WINDOW, :]
    vext_ref[:, 0:HEAD_DIM, 0:WINDOW] = vext_ref[:, 0:HEAD_DIM, ts:ts + WINDOW]
    y_b = _dot_tn(ao_ref[...].astype(BF16), w_attn_ref[...])

    t_a = jnp.concatenate(ta_parts, axis=1)
    t_b = jnp.concatenate(tb_parts, axis=1)
    merged = ((y_a + t_a * y_a) + (y_b + t_b * y_b)).astype(BF16)
    o_ref[0] = x + _dot(merged, w_out_ref[...])


def _ffn_kernel(x_ref, p_ref, g_mlp_ref, w_up_ref, w_down_ref, g_ple_ref, w_pg_ref, w_pp_ref, o_ref):
    x = x_ref[...]
    hm = _rmsnorm(x, g_mlp_ref[...]).astype(BF16)
    acc = x
    for c in range(D_FF // FF_CHUNK):
        u = jnp.maximum(_dot(hm, w_up_ref[:, c * FF_CHUNK:(c + 1) * FF_CHUNK]), 0.0)
        acc = acc + _dot((u * u).astype(BF16), w_down_ref[c * FF_CHUNK:(c + 1) * FF_CHUNK, :])
    hp = _rmsnorm(acc, g_ple_ref[...]).astype(BF16)
    gate = _sigmoid(_dot(hp, w_pg_ref[...]))
    e = _dot(p_ref[...].astype(BF16), w_pp_ref[...])
    o_ref[...] = acc + e * gate


def _resident(shape):
    nd = len(shape)
    return pl.BlockSpec(shape, lambda *_: (0,) * nd, pipeline_mode=pl.Buffered(1))


def _block_diag_tiles(w):
    per = V7X_MXU_DIM // RNN_BLOCK_W
    w4 = w.reshape(GATE_TILES, per, RNN_BLOCK_W, RNN_BLOCK_W)
    eye = jnp.eye(per, dtype=w.dtype)
    return jnp.einsum('cgij,gh->cgihj', w4, eye).reshape(GATE_TILES, V7X_MXU_DIM, V7X_MXU_DIM)


def _rope_gain_tables(seq, gain, scale):
    inv = ROPE_THETA ** (-jnp.arange(0, HEAD_DIM, 2, dtype=F32) / HEAD_DIM)
    ang = inv[:, None] * jnp.arange(seq, dtype=F32)[None, :]
    cos, sin = jnp.cos(ang), jnp.sin(ang)
    g_lo = (gain[:HALF_DIM] * scale)[:, None]
    g_hi = (gain[HALF_DIM:] * scale)[:, None]
    return [g_lo * cos, g_hi * sin, g_hi * cos, g_lo * sin]


def _band_bias_t():
    qi = jnp.arange(WINDOW)[None, :]
    ci = jnp.arange(2 * WINDOW)[:, None]
    diff = WINDOW + qi - ci
    valid = (diff >= 0) & (diff < WINDOW)
    first = valid & (ci >= WINDOW)
    neg = jnp.float32(-jnp.inf)
    return jnp.stack([jnp.where(first, 0.0, neg), jnp.where(valid, 0.0, neg)]).astype(F32)


def _mixer(x, sinks2, mats, vecs):
    batch, seq, _ = x.shape
    ts = SEQ_CHUNK
    assert seq % ts == 0 and ts % WINDOW == 0
    (w_tok, w_qkv_t, wgate, w_rnn, w_attn, w_out) = mats
    (g_mix, conv_w, conv_b, hb_rg, hb_ig, lam, rope, bias) = vecs
    in_specs = [
        pl.BlockSpec((1, ts, D_MODEL), lambda b, j, *_: (b, j, 0)),
        _resident(w_tok.shape), _resident(w_qkv_t.shape), _resident(g_mix.shape), _resident(conv_w.shape),
        _resident(conv_b.shape), _resident(wgate.shape), _resident(hb_rg.shape), _resident(hb_ig.shape),
        _resident(lam.shape), _resident(w_rnn.shape),
        pl.BlockSpec((rope.shape[0], HALF_DIM, ts), lambda b, j, *_: (0, 0, j)),
        _resident(bias.shape), _resident(w_attn.shape), _resident(w_out.shape),
    ]
    grid_spec = pltpu.PrefetchScalarGridSpec(
        num_scalar_prefetch=1,
        grid=(batch, seq // ts),
        in_specs=in_specs,
        out_specs=pl.BlockSpec((1, ts, D_MODEL), lambda b, j, *_: (b, j, 0)),
        scratch_shapes=[
            pltpu.VMEM((ts + V7X_SUBLANES, D_RNN), F32),
            pltpu.VMEM((ts, D_RNN), F32),
            pltpu.VMEM((ts, D_RNN), F32),
            pltpu.VMEM((ts, D_RNN), F32),
            pltpu.VMEM((1, D_RNN), F32),
            pltpu.VMEM((WINDOW + ts, KV_W), BF16),
            pltpu.VMEM((N_KV_HEADS, V_ROWS, WINDOW + ts), BF16),
            pltpu.VMEM((Q_W, ts), F32),
        ],
    )
    return pl.pallas_call(
        _mixer_kernel,
        grid_spec=grid_spec,
        out_shape=jax.ShapeDtypeStruct(x.shape, F32),
        compiler_params=pltpu.CompilerParams(
            dimension_semantics=("arbitrary", "arbitrary"), vmem_limit_bytes=VMEM_LIMIT),
        name="mixer",
    )(sinks2, x, w_tok, w_qkv_t, g_mix, conv_w, conv_b, wgate, hb_rg, hb_ig, lam, w_rnn, rope, bias,
      w_attn, w_out)


def _ffn(x2d, p2d, g_mlp, w_up, w_down, g_ple, w_pg, w_pp):
    rows = x2d.shape[0]
    tm = FFN_ROWS
    assert rows % tm == 0
    return pl.pallas_call(
        _ffn_kernel,
        grid=(rows // tm,),
        in_specs=[
            pl.BlockSpec((tm, D_MODEL), lambda i: (i, 0)),
            pl.BlockSpec((tm, PLE_DIM), lambda i: (i, 0)),
            _resident(g_mlp.shape), _resident(w_up.shape), _resident(w_down.shape),
            _resident(g_ple.shape), _resident(w_pg.shape), _resident(w_pp.shape),
        ],
        out_specs=pl.BlockSpec((tm, D_MODEL), lambda i: (i, 0)),
        out_shape=jax.ShapeDtypeStruct(x2d.shape, F32),
        compiler_params=pltpu.CompilerParams(
            dimension_semantics=("arbitrary",), vmem_limit_bytes=VMEM_LIMIT),
        name="ffn",
    )(x2d, p2d, g_mlp, w_up, w_down, g_ple, w_pg, w_pp)


def kernel(x, p, g_mix, w_in, conv_w, conv_b, w_rg, b_rg, w_ig, b_ig, lru_lambda, w_rnn_proj, q_gain, k_gain, sinks, w_attn_proj, w_out, g_mlp, w_up, w_down, g_ple, w_ple_gate, w_ple_proj):
    batch, seq, _ = x.shape
    depth = p.shape[0]
    bias = _band_bias_t()
    row = lambda v: v.reshape(1, -1)
    for l in range(depth):
        rope = jnp.stack(_rope_gain_tables(seq, q_gain[l], LOG2E * HEAD_DIM ** -0.5)
                         + _rope_gain_tables(seq, k_gain[l], 1.0))
        wl = w_in[l]
        w_tok = jnp.concatenate([wl[:, REF_OFF_XR:REF_OFF_Q], 0.5 * wl[:, REF_OFF_GA:]], axis=1).astype(BF16)
        w_qkv_t = wl[:, REF_OFF_Q:REF_OFF_GA].T.astype(BF16)
        wgate = (0.5 * jnp.concatenate([_block_diag_tiles(w_rg[l]), _block_diag_tiles(w_ig[l])],
                                       axis=2)).astype(BF16)
        mats = (w_tok, w_qkv_t, wgate, (0.5 * w_rnn_proj[l]).astype(BF16),
                w_attn_proj[l].astype(BF16), (0.5 * w_out[l]).astype(BF16))
        vecs = (row(g_mix[l]), conv_w[l], row(conv_b[l]), row(0.5 * b_rg[l]), row(0.5 * b_ig[l]),
                row(lru_lambda[l]), rope, bias)
        x = _mixer(x, sinks[l] * LOG2E, mats, vecs)
        x = _ffn(x.reshape(batch * seq, D_MODEL), p[l].reshape(batch * seq, PLE_DIM), row(g_mlp[l]),
                 w_up[l].astype(BF16), w_down[l].astype(BF16), row(g_ple[l]),
                 w_ple_gate[l].astype(BF16), w_ple_proj[l].astype(BF16)).reshape(batch, seq, D_MODEL)
    return x
```

```python
import math

import jax
import jax.numpy as jnp
import numpy as np
from jax import lax
from jax.experimental import pallas as pl
from jax.experimental.pallas import tpu as pltpu

D_MODEL = 1024
D_RNN = D_MODEL
RNN_BLOCKS = 16
RNN_BLOCK_W = D_RNN // RNN_BLOCKS
CONV_W = 4
LRU_C = 8.0
HEAD_DIM = 64
HALF_DIM = HEAD_DIM // 2
N_Q_HEADS = D_MODEL // HEAD_DIM
N_KV_HEADS = 4
GQA_GROUP = N_Q_HEADS // N_KV_HEADS
WINDOW = 128
ROPE_THETA = 10000.0
Q_W = N_Q_HEADS * HEAD_DIM
KV_W = N_KV_HEADS * HEAD_DIM
QKV_W = Q_W + 2 * KV_W
D_FF = 4 * D_MODEL
PLE_DIM = 256
NORM_EPS = 1e-6
LOG2E = math.log2(math.e)
GELU_C0 = math.sqrt(2.0 / math.pi)
GELU_C1 = GELU_C0 * 0.044715

OFF_XR = 0
OFF_GR = OFF_XR + D_RNN
OFF_Q = OFF_GR + D_RNN
OFF_GA = OFF_Q + QKV_W
OFF_GB = OFF_GA + D_MODEL
IN_TOTAL = OFF_GB + D_MODEL

V7X_LANES = 128
V7X_SUBLANES = 8
V7X_BF16_SUBLANES = 16
V7X_MXU_DIM = 256
V7X_VMEM_BYTES = 64 * 1024 * 1024

GATE_TILES = D_RNN // V7X_MXU_DIM
SEQ_CHUNK = 256
STREAMS = 2
QKV_SPLIT = (512, 512, 256, 256)
FFN_ROWS = 512
FF_CHUNK = 1024
V_ROWS = HEAD_DIM + V7X_BF16_SUBLANES
VMEM_LIMIT = V7X_VMEM_BYTES * 3 // 4

BF16 = jnp.bfloat16
F32 = jnp.float32


def _dot(a, b):
    return jnp.dot(a, b, preferred_element_type=F32)


def _dot_tn(a, b):
    return lax.dot_general(a, b, (((0,), (0,)), ((), ())), preferred_element_type=F32)


def _sigmoid(t):
    return 1.0 / (1.0 + jnp.exp(-t))


def _rmsnorm(t, g):
    ms = jnp.mean(t * t, axis=-1, keepdims=True)
    return t * lax.rsqrt(ms + NORM_EPS) * g


def _head_norm_rope_t(z_t, n_heads, tabs):
    c_lo, s_hi, c_hi, s_lo = tabs
    outs = []
    for hd in range(n_heads):
        t = z_t[hd * HEAD_DIM:(hd + 1) * HEAD_DIM]
        inv = lax.rsqrt(jnp.sum(t * t, axis=0, keepdims=True) * (1.0 / HEAD_DIM) + NORM_EPS)
        t1, t2 = t[:HALF_DIM], t[HALF_DIM:]
        outs.append((t1 * c_lo - t2 * s_hi) * inv)
        outs.append((t2 * c_hi + t1 * s_lo) * inv)
    return jnp.concatenate(outs, axis=0)


def _mixer_kernel(sinks2_ref, x_ref, w_in_ref, g_mix_ref, conv_w_ref, conv_b_ref, wgate_ref,
                  hb_rg_ref, hb_ig_ref, lam_ref, w_rnn_ref, rope_ref, bias_ref, w_attn_ref, w_out_ref,
                  o_ref,
                  ext_ref, xc_ref, a_ref, b_ref, hcar_ref, kext_ref, vext_ref, ao_ref):
    ts = SEQ_CHUNK
    j = pl.program_id(1)

    @pl.when(j == 0)
    def _():
        ext_ref[:, 0:V7X_SUBLANES, :] = jnp.zeros((STREAMS, V7X_SUBLANES, D_RNN), F32)
        hcar_ref[...] = jnp.zeros_like(hcar_ref)
        kext_ref[:, 0:WINDOW, :] = jnp.zeros((STREAMS, WINDOW, KV_W), BF16)
        vext_ref[:, :, 0:HEAD_DIM, 0:WINDOW] = jnp.zeros((STREAMS, N_KV_HEADS, HEAD_DIM, WINDOW), BF16)
        vext_ref[:, :, HEAD_DIM:V_ROWS, :] = jnp.ones(
            (STREAMS, N_KV_HEADS, V_ROWS - HEAD_DIM, WINDOW + ts), BF16)

    lam = lam_ref[...]
    half_c_sp = (0.5 * LRU_C) * (jnp.maximum(-lam, 0.0) + jnp.log1p(jnp.exp(-jnp.abs(lam))))
    cw = [conv_w_ref[jj:jj + 1, :] for jj in range(CONV_W)]
    cb = conv_b_ref[...]

    def stream_phases(s):
        st = {}

        def proj(lo, width):
            return _dot(st['h'], w_in_ref[:, lo:lo + width])

        def p_norm_xr():
            st['h'] = _rmsnorm(x_ref[s], g_mix_ref[...]).astype(BF16)
            ext_ref[s, V7X_SUBLANES:V7X_SUBLANES + ts, :] = proj(OFF_XR, D_RNN)

        def p_conv():
            taps = [ext_ref[s, V7X_SUBLANES - CONV_W + 1 + jj:V7X_SUBLANES - CONV_W + 2 + jj, :]
                    for jj in range(CONV_W - 1)]
            for t in range(ts):
                taps.append(ext_ref[s, V7X_SUBLANES + t:V7X_SUBLANES + t + 1, :])
                acc = cb + taps[0] * cw[0]
                for jj in range(1, CONV_W):
                    acc = acc + taps[jj] * cw[jj]
                xc_ref[s, t:t + 1, :] = acc
                taps = taps[1:]
            ext_ref[s, 0:V7X_SUBLANES, :] = ext_ref[s, ts:ts + V7X_SUBLANES, :]

        def p_gates():
            qkv_lo = OFF_Q
            st['ya'], st['ta'], st['tb'], qkv_parts = [], [], [], []
            for c in range(GATE_TILES):
                sl = slice(c * V7X_MXU_DIM, (c + 1) * V7X_MXU_DIM)
                xcc = xc_ref[s, :, sl]
                gz = _dot(xcc.astype(BF16), wgate_ref[c])
                t_r = jnp.tanh(gz[:, :V7X_MXU_DIM] + hb_rg_ref[:, sl])
                t_i = jnp.tanh(gz[:, V7X_MXU_DIM:] + hb_ig_ref[:, sl])
                hcs = half_c_sp[:, sl]
                neg_log_a = hcs + hcs * t_r
                a = jnp.exp2(neg_log_a * (-LOG2E))
                a_ref[s, :, sl] = a
                y = jnp.tanh(neg_log_a) * (a * a + 1.0)
                xh = 0.5 * xcc
                b_ref[s, :, sl] = jnp.where(y > 0.0, y * lax.rsqrt(y), 0.0) * (xh + xh * t_i)

                gr = proj(OFF_GR + c * V7X_MXU_DIM, V7X_MXU_DIM)
                st['ya'].append((gr, jnp.tanh(gr * (GELU_C0 + GELU_C1 * (gr * gr)))))
                st['ta'].append(jnp.tanh(proj(OFF_GA + c * V7X_MXU_DIM, V7X_MXU_DIM)))
                st['tb'].append(jnp.tanh(proj(OFF_GB + c * V7X_MXU_DIM, V7X_MXU_DIM)))
                qkv_parts.append(proj(qkv_lo, QKV_SPLIT[c]).T)
                qkv_lo += QKV_SPLIT[c]
            st['qkv_t'] = jnp.concatenate(qkv_parts, axis=0)

        def p_qk_prep():
            qkv_t = st.pop('qkv_t')
            st['q_t'] = _head_norm_rope_t(qkv_t[0:Q_W], N_Q_HEADS,
                                          [rope_ref[t] for t in range(4)]).astype(BF16)
            k_t = _head_norm_rope_t(qkv_t[Q_W:Q_W + KV_W], N_KV_HEADS, [rope_ref[4 + t] for t in range(4)])
            kext_ref[s, WINDOW:WINDOW + ts, :] = k_t.T.astype(BF16)
            v_t = qkv_t[Q_W + KV_W:QKV_W].astype(BF16)
            for g in range(N_KV_HEADS):
                vext_ref[s, g, 0:HEAD_DIM, WINDOW:WINDOW + ts] = v_t[g * HEAD_DIM:(g + 1) * HEAD_DIM]

        def p_scan():
            hc = hcar_ref[s]
            for t in range(ts):
                hc = a_ref[s, t:t + 1, :] * hc + b_ref[s, t:t + 1, :]
                b_ref[s, t:t + 1, :] = hc
            hcar_ref[s] = hc
            ya_in = []
            for c in range(GATE_TILES):
                gr, tg = st['ya'][c]
                u = b_ref[s, :, c * V7X_MXU_DIM:(c + 1) * V7X_MXU_DIM] * gr
                ya_in.append((u + u * tg).astype(BF16))
            st['ya'] = jnp.concatenate(ya_in, axis=1)

        def scores(n, g):
            cols = slice(n * WINDOW, (n + 1) * WINDOW)
            q_g = jnp.concatenate([st['q_t'][hd * HEAD_DIM:(hd + 1) * HEAD_DIM, cols]
                                   for hd in range(g * GQA_GROUP, (g + 1) * GQA_GROUP)], axis=1)
            return _dot(kext_ref[s, n * WINDOW:(n + 2) * WINDOW, g * HEAD_DIM:(g + 1) * HEAD_DIM], q_g)

        def attend(n, g, s_t):
            cols = slice(n * WINDOW, (n + 1) * WINDOW)
            bias = bias_ref[1] if n > 0 else bias_ref[jnp.minimum(j, 1)]
            probs, ms = [], []
            for i in range(GQA_GROUP):
                s_i = s_t[:, i * WINDOW:(i + 1) * WINDOW] + bias
                m = jnp.maximum(jnp.max(s_i, axis=0, keepdims=True), sinks2_ref[g * GQA_GROUP + i])
                ms.append(m)
                probs.append(jnp.exp2(s_i - m).astype(BF16))
            o_t = _dot(vext_ref[s, g, :, n * WINDOW:(n + 2) * WINDOW],
                       jnp.concatenate(probs, axis=1))
            for i in range(GQA_GROUP):
                hd = g * GQA_GROUP + i
                blk = slice(i * WINDOW, (i + 1) * WINDOW)
                den = o_t[HEAD_DIM:HEAD_DIM + 1, blk] + jnp.exp2(sinks2_ref[hd] - ms[i])
                ao_ref[s, hd * HEAD_DIM:(hd + 1) * HEAD_DIM, cols] = o_t[0:HEAD_DIM, blk] / den

        def p_attention():
            work = [(n, g) for n in range(ts // WINDOW) for g in range(N_KV_HEADS)]
            s_next = scores(*work[0])
            for idx, (n, g) in enumerate(work):
                s_cur = s_next
                if idx + 1 < len(work):
                    s_next = scores(*work[idx + 1])
                attend(n, g, s_cur)
            kext_ref[s, 0:WINDOW, :] = kext_ref[s, ts:ts + WINDOW, :]
            vext_ref[s, :, 0:HEAD_DIM, 0:WINDOW] = vext_ref[s, :, 0:HEAD_DIM, ts:ts + WINDOW]

        def p_tail():
            y_a = _dot(st['ya'], w_rnn_ref[...])
            y_b = _dot_tn(ao_ref[s].astype(BF16), w_attn_ref[...])
            t_a = jnp.concatenate(st['ta'], axis=1)
            t_b = jnp.concatenate(st['tb'], axis=1)
            merged = ((y_a + t_a * y_a) + (y_b + t_b * y_b)).astype(BF16)
            o_ref[s] = x_ref[s] + _dot(merged, w_out_ref[...])

        return [p_norm_xr, p_conv, p_gates, p_qk_prep, p_scan, p_attention, p_tail]

    streams = [stream_phases(s) for s in range(STREAMS)]
    n_phases = len(streams[0])
    for slot in range(n_phases + STREAMS - 1):
        for s in range(STREAMS):
            k = slot - s
            if 0 <= k < n_phases:
                streams[s][k]()


def _ffn_kernel(x_ref, p_ref, g_mlp_ref, w_up_ref, w_down_ref, g_ple_ref, w_pg_ref, w_pp_ref, o_ref):
    x = x_ref[...]
    hm = _rmsnorm(x, g_mlp_ref[...]).astype(BF16)
    acc = x
    for c in range(D_FF // FF_CHUNK):
        u = jnp.maximum(_dot(hm, w_up_ref[:, c * FF_CHUNK:(c + 1) * FF_CHUNK]), 0.0)
        acc = acc + _dot((u * u).astype(BF16), w_down_ref[c * FF_CHUNK:(c + 1) * FF_CHUNK, :])
    hp = _rmsnorm(acc, g_ple_ref[...]).astype(BF16)
    gate = _sigmoid(_dot(hp, w_pg_ref[...]))
    e = _dot(p_ref[...].astype(BF16), w_pp_ref[...])
    o_ref[...] = acc + e * gate


def _resident(shape):
    nd = len(shape)
    return pl.BlockSpec(shape, lambda *_: (0,) * nd, pipeline_mode=pl.Buffered(1))


def _block_diag_tiles(w):
    per = V7X_MXU_DIM // RNN_BLOCK_W
    w4 = w.reshape(GATE_TILES, per, RNN_BLOCK_W, RNN_BLOCK_W)
    eye = jnp.eye(per, dtype=w.dtype)
    return jnp.einsum('cgij,gh->cgihj', w4, eye).reshape(GATE_TILES, V7X_MXU_DIM, V7X_MXU_DIM)


def _rope_gain_tables(seq, gain, scale):
    inv = ROPE_THETA ** (-np.arange(0, HEAD_DIM, 2, dtype=np.float64) / HEAD_DIM)
    ang = inv[:, None] * np.arange(seq, dtype=np.float64)[None, :]
    cos, sin = jnp.asarray(np.cos(ang), F32), jnp.asarray(np.sin(ang), F32)
    g_lo = (gain[:HALF_DIM] * scale)[:, None]
    g_hi = (gain[HALF_DIM:] * scale)[:, None]
    return [g_lo * cos, g_hi * sin, g_hi * cos, g_lo * sin]


def _band_bias_t():
    qi = jnp.arange(WINDOW)[None, :]
    ci = jnp.arange(2 * WINDOW)[:, None]
    diff = WINDOW + qi - ci
    valid = (diff >= 0) & (diff < WINDOW)
    first = valid & (ci >= WINDOW)
    neg = jnp.float32(-jnp.inf)
    return jnp.stack([jnp.where(first, 0.0, neg), jnp.where(valid, 0.0, neg)]).astype(F32)


def _mixer(x, sinks2, mats, vecs):
    batch, seq, _ = x.shape
    ts = SEQ_CHUNK
    assert seq % ts == 0 and ts % WINDOW == 0 and batch % STREAMS == 0
    (w_in, wgate, w_rnn, w_attn, w_out) = mats
    (g_mix, conv_w, conv_b, hb_rg, hb_ig, lam, rope, bias) = vecs
    in_specs = [
        pl.BlockSpec((STREAMS, ts, D_MODEL), lambda b, j, *_: (b, j, 0)),
        _resident(w_in.shape), _resident(g_mix.shape), _resident(conv_w.shape),
        _resident(conv_b.shape), _resident(wgate.shape), _resident(hb_rg.shape), _resident(hb_ig.shape),
        _resident(lam.shape), _resident(w_rnn.shape),
        pl.BlockSpec((rope.shape[0], HALF_DIM, ts), lambda b, j, *_: (0, 0, j)),
        _resident(bias.shape), _resident(w_attn.shape), _resident(w_out.shape),
    ]
    grid_spec = pltpu.PrefetchScalarGridSpec(
        num_scalar_prefetch=1,
        grid=(batch // STREAMS, seq // ts),
        in_specs=in_specs,
        out_specs=pl.BlockSpec((STREAMS, ts, D_MODEL), lambda b, j, *_: (b, j, 0)),
        scratch_shapes=[
            pltpu.VMEM((STREAMS, ts + V7X_SUBLANES, D_RNN), F32),
            pltpu.VMEM((STREAMS, ts, D_RNN), F32),
            pltpu.VMEM((STREAMS, ts, D_RNN), F32),
            pltpu.VMEM((STREAMS, ts, D_RNN), F32),
            pltpu.VMEM((STREAMS, 1, D_RNN), F32),
            pltpu.VMEM((STREAMS, WINDOW + ts, KV_W), BF16),
            pltpu.VMEM((STREAMS, N_KV_HEADS, V_ROWS, WINDOW + ts), BF16),
            pltpu.VMEM((STREAMS, Q_W, ts), F32),
        ],
    )
    return pl.pallas_call(
        _mixer_kernel,
        grid_spec=grid_spec,
        out_shape=jax.ShapeDtypeStruct(x.shape, F32),
        compiler_params=pltpu.CompilerParams(
            dimension_semantics=("arbitrary", "arbitrary"), vmem_limit_bytes=VMEM_LIMIT),
        name="mixer",
    )(sinks2, x, w_in, g_mix, conv_w, conv_b, wgate, hb_rg, hb_ig, lam, w_rnn, rope, bias,
      w_attn, w_out)


def _ffn(x2d, p2d, g_mlp, w_up, w_down, g_ple, w_pg, w_pp):
    rows = x2d.shape[0]
    tm = FFN_ROWS
    assert rows % tm == 0
    return pl.pallas_call(
        _ffn_kernel,
        grid=(rows // tm,),
        in_specs=[
            pl.BlockSpec((tm, D_MODEL), lambda i: (i, 0)),
            pl.BlockSpec((tm, PLE_DIM), lambda i: (i, 0)),
            _resident(g_mlp.shape), _resident(w_up.shape), _resident(w_down.shape),
            _resident(g_ple.shape), _resident(w_pg.shape), _resident(w_pp.shape),
        ],
        out_specs=pl.BlockSpec((tm, D_MODEL), lambda i: (i, 0)),
        out_shape=jax.ShapeDtypeStruct(x2d.shape, F32),
        compiler_params=pltpu.CompilerParams(
            dimension_semantics=("arbitrary",), vmem_limit_bytes=VMEM_LIMIT),
        name="ffn",
    )(x2d, p2d, g_mlp, w_up, w_down, g_ple, w_pg, w_pp)


def kernel(x, p, g_mix, w_in, conv_w, conv_b, w_rg, b_rg, w_ig, b_ig, lru_lambda, w_rnn_proj, q_gain, k_gain, sinks, w_attn_proj, w_out, g_mlp, w_up, w_down, g_ple, w_ple_gate, w_ple_proj):
    batch, seq, _ = x.shape
    depth = p.shape[0]
    bias = _band_bias_t()
    in_scale = jnp.asarray(np.where(np.arange(IN_TOTAL) >= OFF_GA, 0.5, 1.0), F32)
    row = lambda v: v.reshape(1, -1)
    for l in range(depth):
        rope = jnp.stack(_rope_gain_tables(seq, q_gain[l], LOG2E * HEAD_DIM ** -0.5)
                         + _rope_gain_tables(seq, k_gain[l], 1.0))
        w_in_b = (w_in[l] * in_scale).astype(BF16)
        wgate = (0.5 * jnp.concatenate([_block_diag_tiles(w_rg[l]), _block_diag_tiles(w_ig[l])],
                                       axis=2)).astype(BF16)
        mats = (w_in_b, wgate, (0.5 * w_rnn_proj[l]).astype(BF16),
                w_attn_proj[l].astype(BF16), (0.5 * w_out[l]).astype(BF16))
        vecs = (row(g_mix[l]), conv_w[l], row(conv_b[l]), row(0.5 * b_rg[l]), row(0.5 * b_ig[l]),
                row(lru_lambda[l]), rope, bias)
        x = _mixer(x, sinks[l] * LOG2E, mats, vecs)
        x = _ffn(x.reshape(batch * seq, D_MODEL), p[l].reshape(batch * seq, PLE_DIM), row(g_mlp[l]),
                 w_up[l].astype(BF16), w_down[l].astype(BF16), row(g_ple[l]),
                 w_ple_gate[l].astype(BF16), w_ple_proj[l].astype(BF16)).reshape(batch, seq, D_MODEL)
    return x
```

```python
import math

import jax
import jax.numpy as jnp
import numpy as np
from jax import lax
from jax.experimental import pallas as pl
from jax.experimental.pallas import tpu as pltpu

D_MODEL = 1024
D_RNN = D_MODEL
RNN_BLOCKS = 16
RNN_BLOCK_W = D_RNN // RNN_BLOCKS
CONV_W = 4
LRU_C = 8.0
HEAD_DIM = 64
HALF_DIM = HEAD_DIM // 2
N_Q_HEADS = D_MODEL // HEAD_DIM
N_KV_HEADS = 4
GQA_GROUP = N_Q_HEADS // N_KV_HEADS
WINDOW = 128
ROPE_THETA = 10000.0
Q_W = N_Q_HEADS * HEAD_DIM
KV_W = N_KV_HEADS * HEAD_DIM
QKV_W = Q_W + 2 * KV_W
D_FF = 4 * D_MODEL
PLE_DIM = 256
NORM_EPS = 1e-6
LOG2E = math.log2(math.e)
GELU_C0 = math.sqrt(2.0 / math.pi)
GELU_C1 = GELU_C0 * 0.044715

OFF_XR = 0
OFF_GR = OFF_XR + D_RNN
OFF_Q = OFF_GR + D_RNN
OFF_GA = OFF_Q + QKV_W
OFF_GB = OFF_GA + D_MODEL
IN_TOTAL = OFF_GB + D_MODEL

V7X_LANES = 128
V7X_SUBLANES = 8
V7X_BF16_SUBLANES = 16
V7X_MXU_DIM = 256
V7X_VMEM_BYTES = 64 * 1024 * 1024

GATE_TILES = D_RNN // V7X_MXU_DIM
SEQ_CHUNK = 512
STREAMS = 1
CONV_GROUPS = 2
FFN_ROWS = 1024
FF_CHUNK = 1024
V_ROWS = HEAD_DIM + V7X_BF16_SUBLANES
VMEM_LIMIT = V7X_VMEM_BYTES * 3 // 4

BF16 = jnp.bfloat16
F32 = jnp.float32


def _dot(a, b):
    return jnp.dot(a, b, preferred_element_type=F32)


def _dot_tn(a, b):
    return lax.dot_general(a, b, (((0,), (0,)), ((), ())), preferred_element_type=F32)


def _sigmoid(t):
    return 1.0 / (1.0 + jnp.exp(-t))


def _rmsnorm(t, g):
    ms = jnp.mean(t * t, axis=-1, keepdims=True)
    return t * lax.rsqrt(ms + NORM_EPS) * g


def _head_norm_rope_t(z_t, n_heads, tabs):
    c_lo, s_hi, c_hi, s_lo = tabs
    outs = []
    for hd in range(n_heads):
        t = z_t[hd * HEAD_DIM:(hd + 1) * HEAD_DIM]
        inv = lax.rsqrt(jnp.sum(t * t, axis=0, keepdims=True) * (1.0 / HEAD_DIM) + NORM_EPS)
        t1, t2 = t[:HALF_DIM], t[HALF_DIM:]
        outs.append((t1 * c_lo - t2 * s_hi) * inv)
        outs.append((t2 * c_hi + t1 * s_lo) * inv)
    return jnp.concatenate(outs, axis=0)


def _mixer_kernel(sinks2_ref, x_ref, w_in_ref, g_mix_ref, conv_w_ref, conv_b_ref, wgate_ref,
                  hb_rg_ref, hb_ig_ref, lam_ref, w_rnn_ref, rope_ref, fold_ref, w_attn_ref, w_out_ref,
                  o_ref,
                  ext_ref, xc_ref, a_ref, b_ref, hcar_ref, kext_ref, vext_ref, ao_ref):
    ts = SEQ_CHUNK
    j = pl.program_id(1)

    @pl.when(j == 0)
    def _():
        ext_ref[:, 0:V7X_SUBLANES, :] = jnp.zeros((STREAMS, V7X_SUBLANES, D_RNN), F32)
        hcar_ref[...] = jnp.zeros_like(hcar_ref)
        kext_ref[:, 0:WINDOW, :] = jnp.zeros((STREAMS, WINDOW, KV_W), BF16)
        vext_ref[:, :, 0:HEAD_DIM, 0:WINDOW] = jnp.zeros((STREAMS, N_KV_HEADS, HEAD_DIM, WINDOW), BF16)
        vext_ref[:, :, HEAD_DIM:V_ROWS, :] = jnp.ones(
            (STREAMS, N_KV_HEADS, V_ROWS - HEAD_DIM, WINDOW + ts), BF16)

    lam = lam_ref[...]
    half_c_sp = (0.5 * LRU_C) * (jnp.maximum(-lam, 0.0) + jnp.log1p(jnp.exp(-jnp.abs(lam))))
    cw = [conv_w_ref[jj:jj + 1, :] for jj in range(CONV_W)]
    cb = conv_b_ref[...]
    fold_shape = (WINDOW, GQA_GROUP * WINDOW)
    old_key = (lax.broadcasted_iota(jnp.int32, fold_shape, 0)
               > (lax.broadcasted_iota(jnp.int32, fold_shape, 1) & (WINDOW - 1)))
    no_history = jnp.where(j > 0, 0.0, -jnp.inf).astype(F32)

    def stream_phases(s):
        st = {}

        def proj(lo, width):
            return _dot(st['h'], w_in_ref[:, lo:lo + width])

        def p_norm_xr():
            st['h'] = _rmsnorm(x_ref[s], g_mix_ref[...]).astype(BF16)
            ext_ref[s, V7X_SUBLANES:V7X_SUBLANES + ts, :] = proj(OFF_XR, D_RNN)

        def p_conv():
            taps = [ext_ref[s, V7X_SUBLANES - CONV_W + 1 + jj:V7X_SUBLANES - CONV_W + 2 + jj, :]
                    for jj in range(CONV_W - 1)]
            qkv_parts = []
            rows_per_group = ts // CONV_GROUPS
            cols_per_group = QKV_W // CONV_GROUPS
            for grp in range(CONV_GROUPS):
                for t in range(grp * rows_per_group, (grp + 1) * rows_per_group):
                    taps.append(ext_ref[s, V7X_SUBLANES + t:V7X_SUBLANES + t + 1, :])
                    acc = cb + taps[0] * cw[0]
                    for jj in range(1, CONV_W):
                        acc = acc + taps[jj] * cw[jj]
                    xc_ref[s, t:t + 1, :] = acc
                    taps = taps[1:]
                for lo in range(grp * cols_per_group, (grp + 1) * cols_per_group, V7X_MXU_DIM):
                    qkv_parts.append(proj(OFF_Q + lo, V7X_MXU_DIM).T)
            ext_ref[s, 0:V7X_SUBLANES, :] = ext_ref[s, ts:ts + V7X_SUBLANES, :]
            st['qkv_t'] = qkv_parts

        def p_gates():
            qkv_t = st.pop('qkv_t')
            q_tabs = [rope_ref[t] for t in range(4)]
            k_tabs = [rope_ref[4 + t] for t in range(4)]
            k_tile = qkv_t[Q_W // V7X_MXU_DIM]
            st['ya'], st['ta'], st['tb'], q_parts, k_parts = [], [], [], [], []
            for c in range(GATE_TILES):
                sl = slice(c * V7X_MXU_DIM, (c + 1) * V7X_MXU_DIM)
                xcc = xc_ref[s, :, sl]
                gz = _dot(xcc.astype(BF16), wgate_ref[c])
                t_r = jnp.tanh(gz[:, :V7X_MXU_DIM] + hb_rg_ref[:, sl])
                t_i = jnp.tanh(gz[:, V7X_MXU_DIM:] + hb_ig_ref[:, sl])
                hcs = half_c_sp[:, sl]
                neg_log_a = hcs + hcs * t_r
                a = jnp.exp2(neg_log_a * (-LOG2E))
                a_ref[s, :, sl] = a
                y = jnp.tanh(neg_log_a) * (a * a + 1.0)
                xh = 0.5 * xcc
                b_ref[s, :, sl] = jnp.where(y > 0.0, y * lax.rsqrt(y), 0.0) * (xh + xh * t_i)

                gr = proj(OFF_GR + c * V7X_MXU_DIM, V7X_MXU_DIM)
                st['ya'].append((gr, jnp.tanh(gr * (GELU_C0 + GELU_C1 * (gr * gr)))))
                st['ta'].append(jnp.tanh(proj(OFF_GA + c * V7X_MXU_DIM, V7X_MXU_DIM)))
                st['tb'].append(jnp.tanh(proj(OFF_GB + c * V7X_MXU_DIM, V7X_MXU_DIM)))
                q_parts.append(_head_norm_rope_t(qkv_t[c], V7X_MXU_DIM // HEAD_DIM, q_tabs).astype(BF16))
                k_parts.append(_head_norm_rope_t(k_tile[c * HEAD_DIM:(c + 1) * HEAD_DIM], 1, k_tabs))
            st['q_t'] = jnp.concatenate(q_parts, axis=0)
            st['k_t'] = jnp.concatenate(k_parts, axis=0)
            st['v_t'] = qkv_t[(Q_W + KV_W) // V7X_MXU_DIM]

        def p_qk_prep():
            kext_ref[s, WINDOW:WINDOW + ts, :] = st.pop('k_t').T.astype(BF16)
            v_t = st.pop('v_t').astype(BF16)
            for g in range(N_KV_HEADS):
                vext_ref[s, g, 0:HEAD_DIM, WINDOW:WINDOW + ts] = v_t[g * HEAD_DIM:(g + 1) * HEAD_DIM]

        def p_scan():
            hc = hcar_ref[s]
            for t in range(ts):
                hc = a_ref[s, t:t + 1, :] * hc + b_ref[s, t:t + 1, :]
                b_ref[s, t:t + 1, :] = hc
            hcar_ref[s] = hc
            ya_in = []
            for c in range(GATE_TILES):
                gr, tg = st['ya'][c]
                u = b_ref[s, :, c * V7X_MXU_DIM:(c + 1) * V7X_MXU_DIM] * gr
                ya_in.append((u + u * tg).astype(BF16))
            st['ya'] = jnp.concatenate(ya_in, axis=1)

        def scores(n, g):
            cols = slice(n * WINDOW, (n + 1) * WINDOW)
            q_g = jnp.concatenate([st['q_t'][hd * HEAD_DIM:(hd + 1) * HEAD_DIM, cols]
                                   for hd in range(g * GQA_GROUP, (g + 1) * GQA_GROUP)], axis=1)
            return _dot(kext_ref[s, n * WINDOW:(n + 2) * WINDOW, g * HEAD_DIM:(g + 1) * HEAD_DIM], q_g)

        def softmax(n, g, s_t):
            s_old, s_new = s_t[0:WINDOW], s_t[WINDOW:2 * WINDOW]
            if n == 0:
                s_old = s_old + no_history
            s_win = jnp.where(old_key, s_old, s_new)
            probs, ms = [], []
            for i in range(GQA_GROUP):
                s_i = s_win[:, i * WINDOW:(i + 1) * WINDOW]
                m = jnp.maximum(jnp.max(s_i, axis=0, keepdims=True), sinks2_ref[g * GQA_GROUP + i])
                ms.append(m)
                probs.append(jnp.exp2(s_i - m).astype(BF16))
            p_win = jnp.concatenate(probs, axis=1)
            return jnp.concatenate([p_win * fold_ref[0], p_win * fold_ref[1]], axis=0), ms

        def weighted_values(n, g, p_band, ms):
            cols = slice(n * WINDOW, (n + 1) * WINDOW)
            o_t = _dot(vext_ref[s, g, :, n * WINDOW:(n + 2) * WINDOW], p_band)
            for i in range(GQA_GROUP):
                hd = g * GQA_GROUP + i
                blk = slice(i * WINDOW, (i + 1) * WINDOW)
                den = o_t[HEAD_DIM:HEAD_DIM + 1, blk] + jnp.exp2(sinks2_ref[hd] - ms[i])
                ao_ref[s, hd * HEAD_DIM:(hd + 1) * HEAD_DIM, cols] = o_t[0:HEAD_DIM, blk] / den

        def p_attention():
            work = [(n, g) for n in range(ts // WINDOW) for g in range(N_KV_HEADS)]
            s_q, p_q = {}, {}
            for step in range(len(work) + 2):
                if step < len(work):
                    s_q[step] = scores(*work[step])
                if 0 <= step - 1 < len(work):
                    p_q[step - 1] = softmax(*work[step - 1], s_q.pop(step - 1))
                if 0 <= step - 2 < len(work):
                    weighted_values(*work[step - 2], *p_q.pop(step - 2))
            kext_ref[s, 0:WINDOW, :] = kext_ref[s, ts:ts + WINDOW, :]
            vext_ref[s, :, 0:HEAD_DIM, 0:WINDOW] = vext_ref[s, :, 0:HEAD_DIM, ts:ts + WINDOW]

        def p_tail():
            y_a = _dot(st['ya'], w_rnn_ref[...])
            y_b = _dot_tn(ao_ref[s].astype(BF16), w_attn_ref[...])
            t_a = jnp.concatenate(st['ta'], axis=1)
            t_b = jnp.concatenate(st['tb'], axis=1)
            merged = ((y_a + t_a * y_a) + (y_b + t_b * y_b)).astype(BF16)
            o_ref[s] = x_ref[s] + _dot(merged, w_out_ref[...])

        return [p_norm_xr, p_conv, p_gates, p_qk_prep, p_scan, p_attention, p_tail]

    streams = [stream_phases(s) for s in range(STREAMS)]
    n_phases = len(streams[0])
    for slot in range(n_phases + STREAMS - 1):
        for s in range(STREAMS):
            k = slot - s
            if 0 <= k < n_phases:
                streams[s][k]()


def _ffn_kernel(x_ref, p_ref, g_mlp_ref, w_up_ref, w_down_ref, g_ple_ref, w_pg_ref, w_pp_ref, o_ref):
    x = x_ref[...]
    hm = _rmsnorm(x, g_mlp_ref[...]).astype(BF16)
    acc = x
    for c in range(D_FF // FF_CHUNK):
        u = jnp.maximum(_dot(hm, w_up_ref[:, c * FF_CHUNK:(c + 1) * FF_CHUNK]), 0.0)
        acc = acc + _dot((u * u).astype(BF16), w_down_ref[c * FF_CHUNK:(c + 1) * FF_CHUNK, :])
    hp = _rmsnorm(acc, g_ple_ref[...]).astype(BF16)
    gate = _sigmoid(_dot(hp, w_pg_ref[...]))
    e = _dot(p_ref[...].astype(BF16), w_pp_ref[...])
    o_ref[...] = acc + e * gate


def _resident(shape):
    nd = len(shape)
    return pl.BlockSpec(shape, lambda *_: (0,) * nd, pipeline_mode=pl.Buffered(1))


def _block_diag_tiles(w):
    per = V7X_MXU_DIM // RNN_BLOCK_W
    w4 = w.reshape(GATE_TILES, per, RNN_BLOCK_W, RNN_BLOCK_W)
    eye = jnp.eye(per, dtype=w.dtype)
    return jnp.einsum('cgij,gh->cgihj', w4, eye).reshape(GATE_TILES, V7X_MXU_DIM, V7X_MXU_DIM)


def _rope_gain_tables(seq, gain, scale):
    inv = ROPE_THETA ** (-np.arange(0, HEAD_DIM, 2, dtype=np.float64) / HEAD_DIM)
    ang = inv[:, None] * np.arange(seq, dtype=np.float64)[None, :]
    cos, sin = jnp.asarray(np.cos(ang), F32), jnp.asarray(np.sin(ang), F32)
    g_lo = (gain[:HALF_DIM] * scale)[:, None]
    g_hi = (gain[HALF_DIM:] * scale)[:, None]
    return [g_lo * cos, g_hi * sin, g_hi * cos, g_lo * sin]


def _fold_masks():
    older = np.arange(WINDOW)[:, None] > np.arange(WINDOW)[None, :]
    both = np.stack([older, ~older]).astype(np.float32)
    return jnp.asarray(np.tile(both, (1, 1, GQA_GROUP)), BF16)


def _mixer(x, sinks2, mats, vecs):
    batch, seq, _ = x.shape
    ts = SEQ_CHUNK
    assert seq % ts == 0 and ts % WINDOW == 0 and batch % STREAMS == 0
    (w_in, wgate, w_rnn, w_attn, w_out) = mats
    (g_mix, conv_w, conv_b, hb_rg, hb_ig, lam, rope, fold) = vecs
    in_specs = [
        pl.BlockSpec((STREAMS, ts, D_MODEL), lambda b, j, *_: (b, j, 0)),
        _resident(w_in.shape), _resident(g_mix.shape), _resident(conv_w.shape),
        _resident(conv_b.shape), _resident(wgate.shape), _resident(hb_rg.shape), _resident(hb_ig.shape),
        _resident(lam.shape), _resident(w_rnn.shape),
        pl.BlockSpec((rope.shape[0], HALF_DIM, ts), lambda b, j, *_: (0, 0, j)),
        _resident(fold.shape), _resident(w_attn.shape), _resident(w_out.shape),
    ]
    grid_spec = pltpu.PrefetchScalarGridSpec(
        num_scalar_prefetch=1,
        grid=(batch // STREAMS, seq // ts),
        in_specs=in_specs,
        out_specs=pl.BlockSpec((STREAMS, ts, D_MODEL), lambda b, j, *_: (b, j, 0)),
        scratch_shapes=[
            pltpu.VMEM((STREAMS, ts + V7X_SUBLANES, D_RNN), F32),
            pltpu.VMEM((STREAMS, ts, D_RNN), F32),
            pltpu.VMEM((STREAMS, ts, D_RNN), F32),
            pltpu.VMEM((STREAMS, ts, D_RNN), F32),
            pltpu.VMEM((STREAMS, 1, D_RNN), F32),
            pltpu.VMEM((STREAMS, WINDOW + ts, KV_W), BF16),
            pltpu.VMEM((STREAMS, N_KV_HEADS, V_ROWS, WINDOW + ts), BF16),
            pltpu.VMEM((STREAMS, Q_W, ts), F32),
        ],
    )
    return pl.pallas_call(
        _mixer_kernel,
        grid_spec=grid_spec,
        out_shape=jax.ShapeDtypeStruct(x.shape, F32),
        compiler_params=pltpu.CompilerParams(
            dimension_semantics=("arbitrary", "arbitrary"), vmem_limit_bytes=VMEM_LIMIT),
        name="mixer",
    )(sinks2, x, w_in, g_mix, conv_w, conv_b, wgate, hb_rg, hb_ig, lam, w_rnn, rope, fold,
      w_attn, w_out)


def _ffn(x2d, p2d, g_mlp, w_up, w_down, g_ple, w_pg, w_pp):
    rows = x2d.shape[0]
    tm = FFN_ROWS
    assert rows % tm == 0
    return pl.pallas_call(
        _ffn_kernel,
        grid=(rows // tm,),
        in_specs=[
            pl.BlockSpec((tm, D_MODEL), lambda i: (i, 0)),
            pl.BlockSpec((tm, PLE_DIM), lambda i: (i, 0)),
            _resident(g_mlp.shape), _resident(w_up.shape), _resident(w_down.shape),
            _resident(g_ple.shape), _resident(w_pg.shape), _resident(w_pp.shape),
        ],
        out_specs=pl.BlockSpec((tm, D_MODEL), lambda i: (i, 0)),
        out_shape=jax.ShapeDtypeStruct(x2d.shape, F32),
        compiler_params=pltpu.CompilerParams(
            dimension_semantics=("arbitrary",), vmem_limit_bytes=VMEM_LIMIT),
        name="ffn",
    )(x2d, p2d, g_mlp, w_up, w_down, g_ple, w_pg, w_pp)


def kernel(x, p, g_mix, w_in, conv_w, conv_b, w_rg, b_rg, w_ig, b_ig, lru_lambda, w_rnn_proj, q_gain, k_gain, sinks, w_attn_proj, w_out, g_mlp, w_up, w_down, g_ple, w_ple_gate, w_ple_proj):
    batch, seq, _ = x.shape
    depth = p.shape[0]
    fold = _fold_masks()
    in_scale = jnp.asarray(np.where(np.arange(IN_TOTAL) >= OFF_GA, 0.5, 1.0), F32)
    row = lambda v: v.reshape(1, -1)
    for l in range(depth):
        rope = jnp.stack(_rope_gain_tables(seq, q_gain[l], LOG2E * HEAD_DIM ** -0.5)
                         + _rope_gain_tables(seq, k_gain[l], 1.0))
        w_in_b = (w_in[l] * in_scale).astype(BF16)
        wgate = (0.5 * jnp.concatenate([_block_diag_tiles(w_rg[l]), _block_diag_tiles(w_ig[l])],
                                       axis=2)).astype(BF16)
        mats = (w_in_b, wgate, (0.5 * w_rnn_proj[l]).astype(BF16),
                w_attn_proj[l].astype(BF16), (0.5 * w_out[l]).astype(BF16))
        vecs = (row(g_mix[l]), conv_w[l], row(conv_b[l]), row(0.5 * b_rg[l]), row(0.5 * b_ig[l]),
                row(lru_lambda[l]), rope, fold)
        x = _mixer(x, sinks[l] * LOG2E, mats, vecs)
        x = _ffn(x.reshape(batch * seq, D_MODEL), p[l].reshape(batch * seq, PLE_DIM), row(g_mlp[l]),
                 w_up[l].astype(BF16), w_down[l].astype(BF16), row(g_ple[l]),
                 w_ple_gate[l].astype(BF16), w_ple_proj[l].astype(BF16)).reshape(batch, seq, D_MODEL)
    return x
```

```python
import math

import jax
import jax.numpy as jnp
import numpy as np
from jax import lax
from jax.experimental import pallas as pl
from jax.experimental.pallas import tpu as pltpu

D_MODEL = 1024
D_RNN = D_MODEL
RNN_BLOCKS = 16
RNN_BLOCK_W = D_RNN // RNN_BLOCKS
CONV_W = 4
LRU_C = 8.0
HEAD_DIM = 64
HALF_DIM = HEAD_DIM // 2
N_Q_HEADS = D_MODEL // HEAD_DIM
N_KV_HEADS = 4
GQA_GROUP = N_Q_HEADS // N_KV_HEADS
WINDOW = 128
ROPE_THETA = 10000.0
Q_W = N_Q_HEADS * HEAD_DIM
KV_W = N_KV_HEADS * HEAD_DIM
QKV_W = Q_W + 2 * KV_W
D_FF = 4 * D_MODEL
PLE_DIM = 256
NORM_EPS = 1e-6
LOG2E = math.log2(math.e)
GELU_C0 = math.sqrt(2.0 / math.pi)
GELU_C1 = GELU_C0 * 0.044715

OFF_XR = 0
OFF_GR = OFF_XR + D_RNN
OFF_Q = OFF_GR + D_RNN
OFF_GA = OFF_Q + QKV_W
OFF_GB = OFF_GA + D_MODEL
IN_TOTAL = OFF_GB + D_MODEL

V7X_LANES = 128
V7X_SUBLANES = 8
V7X_BF16_SUBLANES = 16
V7X_MXU_DIM = 256
V7X_VMEM_BYTES = 64 * 1024 * 1024

GATE_TILES = D_RNN // V7X_MXU_DIM
QKV_TILES = QKV_W // V7X_MXU_DIM
SEQ_CHUNK = 512
CONV_GROUPS = 2
FFN_ROWS = 1024
FF_CHUNK = 1024
V_ROWS = HEAD_DIM + V7X_BF16_SUBLANES
VMEM_LIMIT = V7X_VMEM_BYTES * 3 // 4

BF16 = jnp.bfloat16
F32 = jnp.float32


def _dot(a, b):
    return jnp.dot(a, b, preferred_element_type=F32)


def _dot_tn(a, b):
    return lax.dot_general(a, b, (((0,), (0,)), ((), ())), preferred_element_type=F32)


def _sigmoid(t):
    return 1.0 / (1.0 + jnp.exp(-t))


def _rmsnorm(t, g):
    ms = jnp.mean(t * t, axis=-1, keepdims=True)
    return t * lax.rsqrt(ms + NORM_EPS) * g


def _head_norm_rope_t(z_t, n_heads, tabs):
    c_lo, s_hi, c_hi, s_lo = tabs
    outs = []
    for hd in range(n_heads):
        t = z_t[hd * HEAD_DIM:(hd + 1) * HEAD_DIM]
        inv = lax.rsqrt(jnp.sum(t * t, axis=0, keepdims=True) * (1.0 / HEAD_DIM) + NORM_EPS)
        t1, t2 = t[:HALF_DIM], t[HALF_DIM:]
        outs.append((t1 * c_lo - t2 * s_hi) * inv)
        outs.append((t2 * c_hi + t1 * s_lo) * inv)
    return jnp.concatenate(outs, axis=0)


def _mixer_kernel(sinks2_ref, x_ref, w_in_ref, g_mix_ref, conv_w_ref, conv_b_ref, wgate_ref,
                  hb_rg_ref, hb_ig_ref, lam_ref, w_rnn_ref, rope_ref, fold_ref, w_attn_ref, w_out_ref,
                  o_ref,
                  ext_ref, xc_ref, a_ref, b_ref, hcar_ref, kext_ref, vext_ref, ao_ref):
    ts = SEQ_CHUNK
    j = pl.program_id(1)

    @pl.when(j == 0)
    def _():
        ext_ref[0:V7X_SUBLANES, :] = jnp.zeros((V7X_SUBLANES, D_RNN), F32)
        hcar_ref[...] = jnp.zeros_like(hcar_ref)
        kext_ref[0:WINDOW, :] = jnp.zeros((WINDOW, KV_W), BF16)
        vext_ref[:, 0:HEAD_DIM, 0:WINDOW] = jnp.zeros((N_KV_HEADS, HEAD_DIM, WINDOW), BF16)
        vext_ref[:, HEAD_DIM:V_ROWS, :] = jnp.ones((N_KV_HEADS, V_ROWS - HEAD_DIM, WINDOW + ts), BF16)

    h = _rmsnorm(x_ref[0], g_mix_ref[...]).astype(BF16)

    def proj(lo, width):
        return _dot(h, w_in_ref[:, lo:lo + width])

    ext_ref[V7X_SUBLANES:V7X_SUBLANES + ts, :] = proj(OFF_XR, D_RNN)
    cw = [conv_w_ref[jj:jj + 1, :] for jj in range(CONV_W)]
    cb = conv_b_ref[...]
    taps = [ext_ref[V7X_SUBLANES - CONV_W + 1 + jj:V7X_SUBLANES - CONV_W + 2 + jj, :]
            for jj in range(CONV_W - 1)]
    qkv_t = []
    rows_per_group = ts // CONV_GROUPS
    tiles_per_group = QKV_TILES // CONV_GROUPS
    for grp in range(CONV_GROUPS):
        for t in range(grp * rows_per_group, (grp + 1) * rows_per_group):
            taps.append(ext_ref[V7X_SUBLANES + t:V7X_SUBLANES + t + 1, :])
            acc = cb + taps[0] * cw[0]
            for jj in range(1, CONV_W):
                acc = acc + taps[jj] * cw[jj]
            xc_ref[t:t + 1, :] = acc
            taps = taps[1:]
        for tile in range(grp * tiles_per_group, (grp + 1) * tiles_per_group):
            qkv_t.append(proj(OFF_Q + tile * V7X_MXU_DIM, V7X_MXU_DIM).T)
    ext_ref[0:V7X_SUBLANES, :] = ext_ref[ts:ts + V7X_SUBLANES, :]

    lam = lam_ref[...]
    half_c_sp = (0.5 * LRU_C) * (jnp.maximum(-lam, 0.0) + jnp.log1p(jnp.exp(-jnp.abs(lam))))
    q_tabs = [rope_ref[t] for t in range(4)]
    k_tabs = [rope_ref[4 + t] for t in range(4)]
    k_tile = qkv_t[Q_W // V7X_MXU_DIM]
    gelu_parts, ta_parts, tb_parts, q_parts, k_parts = [], [], [], [], []
    for c in range(GATE_TILES):
        sl = slice(c * V7X_MXU_DIM, (c + 1) * V7X_MXU_DIM)
        xcc = xc_ref[:, sl]
        gz = _dot(xcc.astype(BF16), wgate_ref[c])
        t_r = jnp.tanh(gz[:, :V7X_MXU_DIM] + hb_rg_ref[:, sl])
        t_i = jnp.tanh(gz[:, V7X_MXU_DIM:] + hb_ig_ref[:, sl])
        hcs = half_c_sp[:, sl]
        neg_log_a = hcs + hcs * t_r
        a = jnp.exp2(neg_log_a * (-LOG2E))
        a_ref[:, sl] = a
        y = jnp.tanh(neg_log_a) * (a * a + 1.0)
        xh = 0.5 * xcc
        b_ref[:, sl] = jnp.where(y > 0.0, y * lax.rsqrt(y), 0.0) * (xh + xh * t_i)

        gr = proj(OFF_GR + c * V7X_MXU_DIM, V7X_MXU_DIM)
        gelu_parts.append((gr, jnp.tanh(gr * (GELU_C0 + GELU_C1 * (gr * gr)))))
        ta_parts.append(jnp.tanh(proj(OFF_GA + c * V7X_MXU_DIM, V7X_MXU_DIM)))
        tb_parts.append(jnp.tanh(proj(OFF_GB + c * V7X_MXU_DIM, V7X_MXU_DIM)))
        q_parts.append(_head_norm_rope_t(qkv_t[c], V7X_MXU_DIM // HEAD_DIM, q_tabs).astype(BF16))
        k_parts.append(_head_norm_rope_t(k_tile[c * HEAD_DIM:(c + 1) * HEAD_DIM], 1, k_tabs))
    q_t = jnp.concatenate(q_parts, axis=0)
    kext_ref[WINDOW:WINDOW + ts, :] = jnp.concatenate(k_parts, axis=0).T.astype(BF16)
    v_t = qkv_t[(Q_W + KV_W) // V7X_MXU_DIM].astype(BF16)
    for g in range(N_KV_HEADS):
        vext_ref[g, 0:HEAD_DIM, WINDOW:WINDOW + ts] = v_t[g * HEAD_DIM:(g + 1) * HEAD_DIM]

    hc = hcar_ref[...]
    for t in range(ts):
        hc = a_ref[t:t + 1, :] * hc + b_ref[t:t + 1, :]
        b_ref[t:t + 1, :] = hc
    hcar_ref[...] = hc
    ya_in = []
    for c in range(GATE_TILES):
        gr, tg = gelu_parts[c]
        u = b_ref[:, c * V7X_MXU_DIM:(c + 1) * V7X_MXU_DIM] * gr
        ya_in.append((u + u * tg).astype(BF16))
    ya_in = jnp.concatenate(ya_in, axis=1)

    fold_shape = (WINDOW, GQA_GROUP * WINDOW)
    old_key = (lax.broadcasted_iota(jnp.int32, fold_shape, 0)
               > (lax.broadcasted_iota(jnp.int32, fold_shape, 1) & (WINDOW - 1)))
    no_history = jnp.where(j > 0, 0.0, -jnp.inf).astype(F32)

    def scores(n, g):
        cols = slice(n * WINDOW, (n + 1) * WINDOW)
        q_g = jnp.concatenate([q_t[hd * HEAD_DIM:(hd + 1) * HEAD_DIM, cols]
                               for hd in range(g * GQA_GROUP, (g + 1) * GQA_GROUP)], axis=1)
        return _dot(kext_ref[n * WINDOW:(n + 2) * WINDOW, g * HEAD_DIM:(g + 1) * HEAD_DIM], q_g)

    def softmax(n, g, s_t):
        s_old, s_new = s_t[0:WINDOW], s_t[WINDOW:2 * WINDOW]
        if n == 0:
            s_old = s_old + no_history
        s_win = jnp.where(old_key, s_old, s_new)
        probs, ms = [], []
        for i in range(GQA_GROUP):
            s_i = s_win[:, i * WINDOW:(i + 1) * WINDOW]
            m = jnp.maximum(jnp.max(s_i, axis=0, keepdims=True), sinks2_ref[g * GQA_GROUP + i])
            ms.append(m)
            probs.append(jnp.exp2(s_i - m).astype(BF16))
        p_win = jnp.concatenate(probs, axis=1)
        return jnp.concatenate([p_win * fold_ref[0], p_win * fold_ref[1]], axis=0), ms

    def weighted_values(n, g, p_band, ms):
        cols = slice(n * WINDOW, (n + 1) * WINDOW)
        o_t = _dot(vext_ref[g, :, n * WINDOW:(n + 2) * WINDOW], p_band)
        for i in range(GQA_GROUP):
            hd = g * GQA_GROUP + i
            blk = slice(i * WINDOW, (i + 1) * WINDOW)
            den = o_t[HEAD_DIM:HEAD_DIM + 1, blk] + jnp.exp2(sinks2_ref[hd] - ms[i])
            ao_ref[hd * HEAD_DIM:(hd + 1) * HEAD_DIM, cols] = o_t[0:HEAD_DIM, blk] / den

    work = [(n, g) for n in range(ts // WINDOW) for g in range(N_KV_HEADS)]
    s_q, p_q = {}, {}
    for step in range(len(work) + 2):
        if step < len(work):
            s_q[step] = scores(*work[step])
        if 0 <= step - 1 < len(work):
            p_q[step - 1] = softmax(*work[step - 1], s_q.pop(step - 1))
        if 0 <= step - 2 < len(work):
            weighted_values(*work[step - 2], *p_q.pop(step - 2))
    kext_ref[0:WINDOW, :] = kext_ref[ts:ts + WINDOW, :]
    vext_ref[:, 0:HEAD_DIM, 0:WINDOW] = vext_ref[:, 0:HEAD_DIM, ts:ts + WINDOW]

    y_a = _dot(ya_in, w_rnn_ref[...])
    y_b = _dot_tn(ao_ref[...].astype(BF16), w_attn_ref[...])
    t_a = jnp.concatenate(ta_parts, axis=1)
    t_b = jnp.concatenate(tb_parts, axis=1)
    merged = ((y_a + t_a * y_a) + (y_b + t_b * y_b)).astype(BF16)
    o_ref[0] = x_ref[0] + _dot(merged, w_out_ref[...])


def _ffn_kernel(x_ref, p_ref, g_mlp_ref, w_up_ref, w_down_ref, g_ple_ref, w_pg_ref, w_pp_ref, o_ref):
    x = x_ref[...]
    hm = _rmsnorm(x, g_mlp_ref[...]).astype(BF16)
    acc = x
    for c in range(D_FF // FF_CHUNK):
        u = jnp.maximum(_dot(hm, w_up_ref[:, c * FF_CHUNK:(c + 1) * FF_CHUNK]), 0.0)
        acc = acc + _dot((u * u).astype(BF16), w_down_ref[c * FF_CHUNK:(c + 1) * FF_CHUNK, :])
    hp = _rmsnorm(acc, g_ple_ref[...]).astype(BF16)
    gate = _sigmoid(_dot(hp, w_pg_ref[...]))
    e = _dot(p_ref[...].astype(BF16), w_pp_ref[...])
    o_ref[...] = acc + e * gate


def _resident(shape):
    nd = len(shape)
    return pl.BlockSpec(shape, lambda *_: (0,) * nd, pipeline_mode=pl.Buffered(1))


def _block_diag_tiles(w):
    per = V7X_MXU_DIM // RNN_BLOCK_W
    w4 = w.reshape(GATE_TILES, per, RNN_BLOCK_W, RNN_BLOCK_W)
    eye = jnp.eye(per, dtype=w.dtype)
    return jnp.einsum('cgij,gh->cgihj', w4, eye).reshape(GATE_TILES, V7X_MXU_DIM, V7X_MXU_DIM)


def _rope_gain_tables(seq, gain, scale):
    inv = ROPE_THETA ** (-np.arange(0, HEAD_DIM, 2, dtype=np.float64) / HEAD_DIM)
    ang = inv[:, None] * np.arange(seq, dtype=np.float64)[None, :]
    cos, sin = jnp.asarray(np.cos(ang), F32), jnp.asarray(np.sin(ang), F32)
    g_lo = (gain[:HALF_DIM] * scale)[:, None]
    g_hi = (gain[HALF_DIM:] * scale)[:, None]
    return [g_lo * cos, g_hi * sin, g_hi * cos, g_lo * sin]


def _fold_masks():
    older = np.arange(WINDOW)[:, None] > np.arange(WINDOW)[None, :]
    both = np.stack([older, ~older]).astype(np.float32)
    return jnp.asarray(np.tile(both, (1, 1, GQA_GROUP)), BF16)


def _mixer(x, sinks2, mats, vecs):
    batch, seq, _ = x.shape
    ts = SEQ_CHUNK
    assert seq % ts == 0 and ts % (CONV_GROUPS * WINDOW) == 0 and QKV_TILES % CONV_GROUPS == 0
    assert GATE_TILES == N_KV_HEADS == Q_W // V7X_MXU_DIM
    (w_in, wgate, w_rnn, w_attn, w_out) = mats
    (g_mix, conv_w, conv_b, hb_rg, hb_ig, lam, rope, fold) = vecs
    in_specs = [
        pl.BlockSpec((1, ts, D_MODEL), lambda b, j, *_: (b, j, 0)),
        _resident(w_in.shape), _resident(g_mix.shape), _resident(conv_w.shape),
        _resident(conv_b.shape), _resident(wgate.shape), _resident(hb_rg.shape), _resident(hb_ig.shape),
        _resident(lam.shape), _resident(w_rnn.shape),
        pl.BlockSpec((rope.shape[0], HALF_DIM, ts), lambda b, j, *_: (0, 0, j)),
        _resident(fold.shape), _resident(w_attn.shape), _resident(w_out.shape),
    ]
    grid_spec = pltpu.PrefetchScalarGridSpec(
        num_scalar_prefetch=1,
        grid=(batch, seq // ts),
        in_specs=in_specs,
        out_specs=pl.BlockSpec((1, ts, D_MODEL), lambda b, j, *_: (b, j, 0)),
        scratch_shapes=[
            pltpu.VMEM((ts + V7X_SUBLANES, D_RNN), F32),
            pltpu.VMEM((ts, D_RNN), F32),
            pltpu.VMEM((ts, D_RNN), F32),
            pltpu.VMEM((ts, D_RNN), F32),
            pltpu.VMEM((1, D_RNN), F32),
            pltpu.VMEM((WINDOW + ts, KV_W), BF16),
            pltpu.VMEM((N_KV_HEADS, V_ROWS, WINDOW + ts), BF16),
            pltpu.VMEM((Q_W, ts), F32),
        ],
    )
    return pl.pallas_call(
        _mixer_kernel,
        grid_spec=grid_spec,
        out_shape=jax.ShapeDtypeStruct(x.shape, F32),
        compiler_params=pltpu.CompilerParams(
            dimension_semantics=("arbitrary", "arbitrary"), vmem_limit_bytes=VMEM_LIMIT),
        name="mixer",
    )(sinks2, x, w_in, g_mix, conv_w, conv_b, wgate, hb_rg, hb_ig, lam, w_rnn, rope, fold,
      w_attn, w_out)


def _ffn(x2d, p2d, g_mlp, w_up, w_down, g_ple, w_pg, w_pp):
    rows = x2d.shape[0]
    tm = FFN_ROWS
    assert rows % tm == 0
    return pl.pallas_call(
        _ffn_kernel,
        grid=(rows // tm,),
        in_specs=[
            pl.BlockSpec((tm, D_MODEL), lambda i: (i, 0)),
            pl.BlockSpec((tm, PLE_DIM), lambda i: (i, 0)),
            _resident(g_mlp.shape), _resident(w_up.shape), _resident(w_down.shape),
            _resident(g_ple.shape), _resident(w_pg.shape), _resident(w_pp.shape),
        ],
        out_specs=pl.BlockSpec((tm, D_MODEL), lambda i: (i, 0)),
        out_shape=jax.ShapeDtypeStruct(x2d.shape, F32),
        compiler_params=pltpu.CompilerParams(
            dimension_semantics=("arbitrary",), vmem_limit_bytes=VMEM_LIMIT),
        name="ffn",
    )(x2d, p2d, g_mlp, w_up, w_down, g_ple, w_pg, w_pp)


def kernel(x, p, g_mix, w_in, conv_w, conv_b, w_rg, b_rg, w_ig, b_ig, lru_lambda, w_rnn_proj, q_gain, k_gain, sinks, w_attn_proj, w_out, g_mlp, w_up, w_down, g_ple, w_ple_gate, w_ple_proj):
    batch, seq, _ = x.shape
    depth = p.shape[0]
    fold = _fold_masks()
    in_scale = jnp.asarray(np.where(np.arange(IN_TOTAL) >= OFF_GA, 0.5, 1.0), F32)
    row = lambda v: v.reshape(1, -1)
    for l in range(depth):
        rope = jnp.stack(_rope_gain_tables(seq, q_gain[l], LOG2E * HEAD_DIM ** -0.5)
                         + _rope_gain_tables(seq, k_gain[l], 1.0))
        w_in_b = (w_in[l] * in_scale).astype(BF16)
        wgate = (0.5 * jnp.concatenate([_block_diag_tiles(w_rg[l]), _block_diag_tiles(w_ig[l])],
                                       axis=2)).astype(BF16)
        mats = (w_in_b, wgate, (0.5 * w_rnn_proj[l]).astype(BF16),
                w_attn_proj[l].astype(BF16), (0.5 * w_out[l]).astype(BF16))
        vecs = (row(g_mix[l]), conv_w[l], row(conv_b[l]), row(0.5 * b_rg[l]), row(0.5 * b_ig[l]),
                row(lru_lambda[l]), rope, fold)
        x = _mixer(x, sinks[l] * LOG2E, mats, vecs)
        x = _ffn(x.reshape(batch * seq, D_MODEL), p[l].reshape(batch * seq, PLE_DIM), row(g_mlp[l]),
                 w_up[l].astype(BF16), w_down[l].astype(BF16), row(g_ple[l]),
                 w_ple_gate[l].astype(BF16), w_ple_proj[l].astype(BF16)).reshape(batch, seq, D_MODEL)
    return x
```

```python
import math

import jax
import jax.numpy as jnp
import numpy as np
from jax import lax
from jax.experimental import pallas as pl
from jax.experimental.pallas import tpu as pltpu

D_MODEL = 1024
D_RNN = D_MODEL
RNN_BLOCKS = 16
RNN_BLOCK_W = D_RNN // RNN_BLOCKS
CONV_W = 4
LRU_C = 8.0
HEAD_DIM = 64
HALF_DIM = HEAD_DIM // 2
N_Q_HEADS = D_MODEL // HEAD_DIM
N_KV_HEADS = 4
GQA_GROUP = N_Q_HEADS // N_KV_HEADS
WINDOW = 128
ROPE_THETA = 10000.0
Q_W = N_Q_HEADS * HEAD_DIM
KV_W = N_KV_HEADS * HEAD_DIM
QKV_W = Q_W + 2 * KV_W
D_FF = 4 * D_MODEL
PLE_DIM = 256
NORM_EPS = 1e-6
LOG2E = math.log2(math.e)
GELU_C0 = math.sqrt(2.0 / math.pi)
GELU_C1 = GELU_C0 * 0.044715

OFF_XR = 0
OFF_GR = OFF_XR + D_RNN
OFF_Q = OFF_GR + D_RNN
OFF_GA = OFF_Q + QKV_W
OFF_GB = OFF_GA + D_MODEL
IN_TOTAL = OFF_GB + D_MODEL

V7X_LANES = 128
V7X_SUBLANES = 8
V7X_BF16_SUBLANES = 16
V7X_MXU_DIM = 256
V7X_VMEM_BYTES = 64 * 1024 * 1024

GATE_TILES = D_RNN // V7X_MXU_DIM
QKV_TILES = QKV_W // V7X_MXU_DIM
SEQ_CHUNK = 512
CONV_GROUPS = 2
FFN_ROWS = 1024
FF_CHUNK = 1024
V_ROWS = HEAD_DIM + V7X_BF16_SUBLANES
VMEM_LIMIT = V7X_VMEM_BYTES * 3 // 4

BF16 = jnp.bfloat16
F32 = jnp.float32


def _dot(a, b):
    return jnp.dot(a, b, preferred_element_type=F32)


def _dot_tn(a, b):
    return lax.dot_general(a, b, (((0,), (0,)), ((), ())), preferred_element_type=F32)


def _rmsnorm(t, g):
    ms = jnp.mean(t * t, axis=-1, keepdims=True)
    return t * lax.rsqrt(ms + NORM_EPS) * g


def _head_norm_rope_t(z_t, n_heads, tabs):
    c_lo, s_hi, c_hi, s_lo = tabs
    outs = []
    for hd in range(n_heads):
        t = z_t[hd * HEAD_DIM:(hd + 1) * HEAD_DIM]
        inv = lax.rsqrt(jnp.sum(t * t, axis=0, keepdims=True) * (1.0 / HEAD_DIM) + NORM_EPS)
        t1, t2 = t[:HALF_DIM], t[HALF_DIM:]
        outs.append((t1 * c_lo - t2 * s_hi) * inv)
        outs.append((t2 * c_hi + t1 * s_lo) * inv)
    return jnp.concatenate(outs, axis=0)


def _mixer_kernel(sinks2_ref, x_ref, w_in_ref, g_mix_ref, conv_w_ref, conv_b_ref, wgate_ref,
                  hb_rg_ref, hb_ig_ref, lam_ref, w_rnn_ref, rope_ref, fold_ref, w_attn_ref, w_out_ref,
                  o_ref,
                  ext_ref, xc_ref, a_ref, b_ref, hcar_ref, kext_ref, vext_ref, ao_ref):
    ts = SEQ_CHUNK
    j = pl.program_id(1)

    @pl.when(j == 0)
    def _():
        ext_ref[0:V7X_SUBLANES, :] = jnp.zeros((V7X_SUBLANES, D_RNN), F32)
        hcar_ref[...] = jnp.zeros_like(hcar_ref)
        kext_ref[0:WINDOW, :] = jnp.zeros((WINDOW, KV_W), BF16)
        vext_ref[:, 0:HEAD_DIM, 0:WINDOW] = jnp.zeros((N_KV_HEADS, HEAD_DIM, WINDOW), BF16)
        vext_ref[:, HEAD_DIM:V_ROWS, :] = jnp.ones((N_KV_HEADS, V_ROWS - HEAD_DIM, WINDOW + ts), BF16)

    h = _rmsnorm(x_ref[0], g_mix_ref[...]).astype(BF16)

    def proj(lo, width):
        return _dot(h, w_in_ref[:, lo:lo + width])

    ext_ref[V7X_SUBLANES:V7X_SUBLANES + ts, :] = proj(OFF_XR, D_RNN)
    cw = [conv_w_ref[jj:jj + 1, :] for jj in range(CONV_W)]
    cb = conv_b_ref[...]
    taps = [ext_ref[V7X_SUBLANES - CONV_W + 1 + jj:V7X_SUBLANES - CONV_W + 2 + jj, :]
            for jj in range(CONV_W - 1)]
    qkv_t = []
    rows_per_group = ts // CONV_GROUPS
    tiles_per_group = QKV_TILES // CONV_GROUPS
    for grp in range(CONV_GROUPS):
        for t in range(grp * rows_per_group, (grp + 1) * rows_per_group):
            taps.append(ext_ref[V7X_SUBLANES + t:V7X_SUBLANES + t + 1, :])
            acc = cb + taps[0] * cw[0]
            for jj in range(1, CONV_W):
                acc = acc + taps[jj] * cw[jj]
            xc_ref[t:t + 1, :] = acc
            taps = taps[1:]
        for tile in range(grp * tiles_per_group, (grp + 1) * tiles_per_group):
            qkv_t.append(proj(OFF_Q + tile * V7X_MXU_DIM, V7X_MXU_DIM).T)
    ext_ref[0:V7X_SUBLANES, :] = ext_ref[ts:ts + V7X_SUBLANES, :]

    lam = lam_ref[...]
    half_c_sp = (0.5 * LRU_C) * (jnp.maximum(-lam, 0.0) + jnp.log1p(jnp.exp(-jnp.abs(lam))))
    q_tabs = [rope_ref[t] for t in range(4)]
    k_tabs = [rope_ref[4 + t] for t in range(4)]
    k_tile = qkv_t[Q_W // V7X_MXU_DIM]
    gelu_parts, ta_parts, tb_parts, q_parts, k_parts = [], [], [], [], []
    for c in range(GATE_TILES):
        sl = slice(c * V7X_MXU_DIM, (c + 1) * V7X_MXU_DIM)
        xcc = xc_ref[:, sl]
        gz = _dot(xcc.astype(BF16), wgate_ref[c])
        t_r = jnp.tanh(gz[:, :V7X_MXU_DIM] + hb_rg_ref[:, sl])
        t_i = jnp.tanh(gz[:, V7X_MXU_DIM:] + hb_ig_ref[:, sl])
        hcs = half_c_sp[:, sl]
        neg_log_a = hcs + hcs * t_r
        a = jnp.exp2(neg_log_a * (-LOG2E))
        a_ref[:, sl] = a
        y = jnp.tanh(neg_log_a) * (a * a + 1.0)
        xh = 0.5 * xcc
        b_ref[:, sl] = jnp.where(y > 0.0, y * lax.rsqrt(y), 0.0) * (xh + xh * t_i)

        gr = proj(OFF_GR + c * V7X_MXU_DIM, V7X_MXU_DIM)
        gelu_parts.append((gr, jnp.tanh(gr * (GELU_C0 + GELU_C1 * (gr * gr)))))
        ta_parts.append(jnp.tanh(proj(OFF_GA + c * V7X_MXU_DIM, V7X_MXU_DIM)))
        tb_parts.append(jnp.tanh(proj(OFF_GB + c * V7X_MXU_DIM, V7X_MXU_DIM)))
        q_parts.append(_head_norm_rope_t(qkv_t[c], V7X_MXU_DIM // HEAD_DIM, q_tabs).astype(BF16))
        k_parts.append(_head_norm_rope_t(k_tile[c * HEAD_DIM:(c + 1) * HEAD_DIM], 1, k_tabs))
    q_t = jnp.concatenate(q_parts, axis=0)
    kext_ref[WINDOW:WINDOW + ts, :] = jnp.concatenate(k_parts, axis=0).T.astype(BF16)
    v_t = qkv_t[(Q_W + KV_W) // V7X_MXU_DIM].astype(BF16)
    for g in range(N_KV_HEADS):
        vext_ref[g, 0:HEAD_DIM, WINDOW:WINDOW + ts] = v_t[g * HEAD_DIM:(g + 1) * HEAD_DIM]

    hc = hcar_ref[...]
    for t in range(ts):
        hc = a_ref[t:t + 1, :] * hc + b_ref[t:t + 1, :]
        b_ref[t:t + 1, :] = hc
    hcar_ref[...] = hc
    ya_in = []
    for c in range(GATE_TILES):
        gr, tg = gelu_parts[c]
        u = b_ref[:, c * V7X_MXU_DIM:(c + 1) * V7X_MXU_DIM] * gr
        ya_in.append((u + u * tg).astype(BF16))
    ya_in = jnp.concatenate(ya_in, axis=1)

    fold_shape = (WINDOW, GQA_GROUP * WINDOW)
    old_key = (lax.broadcasted_iota(jnp.int32, fold_shape, 0)
               > (lax.broadcasted_iota(jnp.int32, fold_shape, 1) & (WINDOW - 1)))
    no_history = jnp.where(j > 0, 0.0, -jnp.inf).astype(F32)

    def scores(n, g):
        cols = slice(n * WINDOW, (n + 1) * WINDOW)
        q_g = jnp.concatenate([q_t[hd * HEAD_DIM:(hd + 1) * HEAD_DIM, cols]
                               for hd in range(g * GQA_GROUP, (g + 1) * GQA_GROUP)], axis=1)
        return _dot(kext_ref[n * WINDOW:(n + 2) * WINDOW, g * HEAD_DIM:(g + 1) * HEAD_DIM], q_g)

    def softmax(n, g, s_t):
        s_old, s_new = s_t[0:WINDOW], s_t[WINDOW:2 * WINDOW]
        if n == 0:
            s_old = s_old + no_history
        s_win = jnp.where(old_key, s_old, s_new)
        probs, ms = [], []
        for i in range(GQA_GROUP):
            s_i = s_win[:, i * WINDOW:(i + 1) * WINDOW]
            m = jnp.maximum(jnp.max(s_i, axis=0, keepdims=True), sinks2_ref[g * GQA_GROUP + i])
            ms.append(m)
            probs.append(jnp.exp2(s_i - m).astype(BF16))
        p_win = jnp.concatenate(probs, axis=1)
        return jnp.concatenate([p_win * fold_ref[0], p_win * fold_ref[1]], axis=0), ms

    def weighted_values(n, g, p_band, ms):
        cols = slice(n * WINDOW, (n + 1) * WINDOW)
        o_t = _dot(vext_ref[g, :, n * WINDOW:(n + 2) * WINDOW], p_band)
        for i in range(GQA_GROUP):
            hd = g * GQA_GROUP + i
            blk = slice(i * WINDOW, (i + 1) * WINDOW)
            den = o_t[HEAD_DIM:HEAD_DIM + 1, blk] + jnp.exp2(sinks2_ref[hd] - ms[i])
            ao_ref[hd * HEAD_DIM:(hd + 1) * HEAD_DIM, cols] = o_t[0:HEAD_DIM, blk] / den

    work = [(n, g) for n in range(ts // WINDOW) for g in range(N_KV_HEADS)]
    s_q, p_q = {}, {}
    for step in range(len(work) + 2):
        if step < len(work):
            s_q[step] = scores(*work[step])
        if 0 <= step - 1 < len(work):
            p_q[step - 1] = softmax(*work[step - 1], s_q.pop(step - 1))
        if 0 <= step - 2 < len(work):
            weighted_values(*work[step - 2], *p_q.pop(step - 2))
    kext_ref[0:WINDOW, :] = kext_ref[ts:ts + WINDOW, :]
    vext_ref[:, 0:HEAD_DIM, 0:WINDOW] = vext_ref[:, 0:HEAD_DIM, ts:ts + WINDOW]

    y_a = _dot(ya_in, w_rnn_ref[...])
    y_b = _dot_tn(ao_ref[...].astype(BF16), w_attn_ref[...])
    t_a = jnp.concatenate(ta_parts, axis=1)
    t_b = jnp.concatenate(tb_parts, axis=1)
    merged = ((y_a + t_a * y_a) + (y_b + t_b * y_b)).astype(BF16)
    o_ref[0] = x_ref[0] + _dot(merged, w_out_ref[...])


def _ffn_kernel(x_ref, p_ref, g_mlp_ref, w_up_ref, w_down_ref, g_ple_ref, w_pg_ref, w_pp_ref, o_ref):
    x = x_ref[...]
    hm = _rmsnorm(x, g_mlp_ref[...]).astype(BF16)
    acc = x
    for c in range(D_FF // FF_CHUNK):
        u = jnp.maximum(_dot(hm, w_up_ref[:, c * FF_CHUNK:(c + 1) * FF_CHUNK]), 0.0)
        acc = acc + _dot((u * u).astype(BF16), w_down_ref[c * FF_CHUNK:(c + 1) * FF_CHUNK, :])
    hp = _rmsnorm(acc, g_ple_ref[...]).astype(BF16)
    pb = p_ref[...].astype(BF16)
    for c in range(D_MODEL // V7X_MXU_DIM):
        sl = slice(c * V7X_MXU_DIM, (c + 1) * V7X_MXU_DIM)
        t_g = jnp.tanh(_dot(hp, w_pg_ref[:, sl]))
        e_h = _dot(pb, w_pp_ref[:, sl])
        o_ref[:, sl] = acc[:, sl] + (e_h + e_h * t_g)


def _resident(shape):
    nd = len(shape)
    return pl.BlockSpec(shape, lambda *_: (0,) * nd, pipeline_mode=pl.Buffered(1))


def _block_diag_tiles(w):
    per = V7X_MXU_DIM // RNN_BLOCK_W
    w4 = w.reshape(GATE_TILES, per, RNN_BLOCK_W, RNN_BLOCK_W)
    eye = jnp.eye(per, dtype=w.dtype)
    return jnp.einsum('cgij,gh->cgihj', w4, eye).reshape(GATE_TILES, V7X_MXU_DIM, V7X_MXU_DIM)


def _rope_gain_tables(seq, gain, scale):
    inv = ROPE_THETA ** (-np.arange(0, HEAD_DIM, 2, dtype=np.float64) / HEAD_DIM)
    ang = inv[:, None] * np.arange(seq, dtype=np.float64)[None, :]
    cos, sin = jnp.asarray(np.cos(ang), F32), jnp.asarray(np.sin(ang), F32)
    g_lo = (gain[:HALF_DIM] * scale)[:, None]
    g_hi = (gain[HALF_DIM:] * scale)[:, None]
    return [g_lo * cos, g_hi * sin, g_hi * cos, g_lo * sin]


def _fold_masks():
    older = np.arange(WINDOW)[:, None] > np.arange(WINDOW)[None, :]
    both = np.stack([older, ~older]).astype(np.float32)
    return jnp.asarray(np.tile(both, (1, 1, GQA_GROUP)), BF16)


def _mixer(x, sinks2, mats, vecs):
    batch, seq, _ = x.shape
    ts = SEQ_CHUNK
    assert seq % ts == 0 and ts % (CONV_GROUPS * WINDOW) == 0 and QKV_TILES % CONV_GROUPS == 0
    assert GATE_TILES == N_KV_HEADS == Q_W // V7X_MXU_DIM
    (w_in, wgate, w_rnn, w_attn, w_out) = mats
    (g_mix, conv_w, conv_b, hb_rg, hb_ig, lam, rope, fold) = vecs
    in_specs = [
        pl.BlockSpec((1, ts, D_MODEL), lambda b, j, *_: (b, j, 0)),
        _resident(w_in.shape), _resident(g_mix.shape), _resident(conv_w.shape),
        _resident(conv_b.shape), _resident(wgate.shape), _resident(hb_rg.shape), _resident(hb_ig.shape),
        _resident(lam.shape), _resident(w_rnn.shape),
        pl.BlockSpec((rope.shape[0], HALF_DIM, ts), lambda b, j, *_: (0, 0, j)),
        _resident(fold.shape), _resident(w_attn.shape), _resident(w_out.shape),
    ]
    grid_spec = pltpu.PrefetchScalarGridSpec(
        num_scalar_prefetch=1,
        grid=(batch, seq // ts),
        in_specs=in_specs,
        out_specs=pl.BlockSpec((1, ts, D_MODEL), lambda b, j, *_: (b, j, 0)),
        scratch_shapes=[
            pltpu.VMEM((ts + V7X_SUBLANES, D_RNN), F32),
            pltpu.VMEM((ts, D_RNN), F32),
            pltpu.VMEM((ts, D_RNN), F32),
            pltpu.VMEM((ts, D_RNN), F32),
            pltpu.VMEM((1, D_RNN), F32),
            pltpu.VMEM((WINDOW + ts, KV_W), BF16),
            pltpu.VMEM((N_KV_HEADS, V_ROWS, WINDOW + ts), BF16),
            pltpu.VMEM((Q_W, ts), F32),
        ],
    )
    return pl.pallas_call(
        _mixer_kernel,
        grid_spec=grid_spec,
        out_shape=jax.ShapeDtypeStruct(x.shape, F32),
        compiler_params=pltpu.CompilerParams(
            dimension_semantics=("arbitrary", "arbitrary"), vmem_limit_bytes=VMEM_LIMIT),
        name="mixer",
    )(sinks2, x, w_in, g_mix, conv_w, conv_b, wgate, hb_rg, hb_ig, lam, w_rnn, rope, fold,
      w_attn, w_out)


def _ffn(x2d, p2d, g_mlp, w_up, w_down, g_ple, w_pg, w_pp):
    rows = x2d.shape[0]
    tm = FFN_ROWS
    assert rows % tm == 0
    return pl.pallas_call(
        _ffn_kernel,
        grid=(rows // tm,),
        in_specs=[
            pl.BlockSpec((tm, D_MODEL), lambda i: (i, 0)),
            pl.BlockSpec((tm, PLE_DIM), lambda i: (i, 0)),
            _resident(g_mlp.shape), _resident(w_up.shape), _resident(w_down.shape),
            _resident(g_ple.shape), _resident(w_pg.shape), _resident(w_pp.shape),
        ],
        out_specs=pl.BlockSpec((tm, D_MODEL), lambda i: (i, 0)),
        out_shape=jax.ShapeDtypeStruct(x2d.shape, F32),
        compiler_params=pltpu.CompilerParams(
            dimension_semantics=("arbitrary",), vmem_limit_bytes=VMEM_LIMIT),
        name="ffn",
    )(x2d, p2d, g_mlp, w_up, w_down, g_ple, w_pg, w_pp)


def kernel(x, p, g_mix, w_in, conv_w, conv_b, w_rg, b_rg, w_ig, b_ig, lru_lambda, w_rnn_proj, q_gain, k_gain, sinks, w_attn_proj, w_out, g_mlp, w_up, w_down, g_ple, w_ple_gate, w_ple_proj):
    batch, seq, _ = x.shape
    depth = p.shape[0]
    fold = _fold_masks()
    in_scale = jnp.asarray(np.where(np.arange(IN_TOTAL) >= OFF_GA, 0.5, 1.0), F32)
    row = lambda v: v.reshape(1, -1)
    for l in range(depth):
        rope = jnp.stack(_rope_gain_tables(seq, q_gain[l], LOG2E * HEAD_DIM ** -0.5)
                         + _rope_gain_tables(seq, k_gain[l], 1.0))
        w_in_b = (w_in[l] * in_scale).astype(BF16)
        wgate = (0.5 * jnp.concatenate([_block_diag_tiles(w_rg[l]), _block_diag_tiles(w_ig[l])],
                                       axis=2)).astype(BF16)
        mats = (w_in_b, wgate, (0.5 * w_rnn_proj[l]).astype(BF16),
                w_attn_proj[l].astype(BF16), (0.5 * w_out[l]).astype(BF16))
        vecs = (row(g_mix[l]), conv_w[l], row(conv_b[l]), row(0.5 * b_rg[l]), row(0.5 * b_ig[l]),
                row(lru_lambda[l]), rope, fold)
        x = _mixer(x, sinks[l] * LOG2E, mats, vecs)
        x = _ffn(x.reshape(batch * seq, D_MODEL), p[l].reshape(batch * seq, PLE_DIM), row(g_mlp[l]),
                 w_up[l].astype(BF16), w_down[l].astype(BF16), row(g_ple[l]),
                 (0.5 * w_ple_gate[l]).astype(BF16),
                 (0.5 * w_ple_proj[l]).astype(BF16)).reshape(batch, seq, D_MODEL)
    return x
```

```python
import math

import jax
import jax.numpy as jnp
import numpy as np
from jax import lax
from jax.experimental import pallas as pl
from jax.experimental.pallas import tpu as pltpu

D_MODEL = 1024
D_RNN = D_MODEL
RNN_BLOCKS = 16
RNN_BLOCK_W = D_RNN // RNN_BLOCKS
CONV_W = 4
LRU_C = 8.0
HEAD_DIM = 64
HALF_DIM = HEAD_DIM // 2
N_Q_HEADS = D_MODEL // HEAD_DIM
N_KV_HEADS = 4
GQA_GROUP = N_Q_HEADS // N_KV_HEADS
WINDOW = 128
ROPE_THETA = 10000.0
Q_W = N_Q_HEADS * HEAD_DIM
KV_W = N_KV_HEADS * HEAD_DIM
QKV_W = Q_W + 2 * KV_W
D_FF = 4 * D_MODEL
PLE_DIM = 256
NORM_EPS = 1e-6
LOG2E = math.log2(math.e)
GELU_C0 = math.sqrt(2.0 / math.pi)
GELU_C1 = GELU_C0 * 0.044715

OFF_XR = 0
OFF_GR = OFF_XR + D_RNN
OFF_Q = OFF_GR + D_RNN
OFF_GA = OFF_Q + QKV_W
OFF_GB = OFF_GA + D_MODEL
IN_TOTAL = OFF_GB + D_MODEL

V7X_LANES = 128
V7X_SUBLANES = 8
V7X_BF16_SUBLANES = 16
V7X_MXU_DIM = 256
V7X_VMEM_BYTES = 64 * 1024 * 1024

GATE_TILES = D_RNN // V7X_MXU_DIM
QKV_TILES = QKV_W // V7X_MXU_DIM
SEQ_CHUNK = 512
CONV_GROUPS = 2
FFN_ROWS = 1024
FF_CHUNK = 1024
V_ROWS = HEAD_DIM + V7X_BF16_SUBLANES
VMEM_LIMIT = V7X_VMEM_BYTES * 3 // 4

BF16 = jnp.bfloat16
F32 = jnp.float32


def _dot(a, b):
    return jnp.dot(a, b, preferred_element_type=F32)


def _dot_tn(a, b):
    return lax.dot_general(a, b, (((0,), (0,)), ((), ())), preferred_element_type=F32)


def _rmsnorm(t, g):
    ms = jnp.mean(t * t, axis=-1, keepdims=True)
    return t * lax.rsqrt(ms + NORM_EPS) * g


def _head_norm_rope_t(z_t, n_heads, tabs):
    c_lo, s_hi, c_hi, s_lo = tabs
    outs = []
    for hd in range(n_heads):
        t = z_t[hd * HEAD_DIM:(hd + 1) * HEAD_DIM]
        inv = lax.rsqrt(jnp.sum(t * t, axis=0, keepdims=True) * (1.0 / HEAD_DIM) + NORM_EPS)
        t1, t2 = t[:HALF_DIM], t[HALF_DIM:]
        outs.append((t1 * c_lo - t2 * s_hi) * inv)
        outs.append((t2 * c_hi + t1 * s_lo) * inv)
    return jnp.concatenate(outs, axis=0)


def _mixer_kernel(sinks2_ref, x_ref, w_in_ref, g_mix_ref, conv_w_ref, conv_b_ref, wgate_ref,
                  hb_rg_ref, hb_ig_ref, lam_ref, w_rnn_ref, rope_ref, fold_ref, w_attn_ref, w_out_ref,
                  o_ref,
                  ext_ref, xc_ref, a_ref, b_ref, hcar_ref, kext_ref, vext_ref, ao_ref):
    ts = SEQ_CHUNK
    j = pl.program_id(1)

    @pl.when(j == 0)
    def _():
        ext_ref[0:V7X_SUBLANES, :] = jnp.zeros((V7X_SUBLANES, D_RNN), F32)
        hcar_ref[...] = jnp.zeros_like(hcar_ref)
        kext_ref[0:WINDOW, :] = jnp.zeros((WINDOW, KV_W), BF16)
        vext_ref[:, 0:HEAD_DIM, 0:WINDOW] = jnp.zeros((N_KV_HEADS, HEAD_DIM, WINDOW), BF16)
        vext_ref[:, HEAD_DIM:V_ROWS, :] = jnp.ones((N_KV_HEADS, V_ROWS - HEAD_DIM, WINDOW + ts), BF16)

    h = _rmsnorm(x_ref[0], g_mix_ref[...]).astype(BF16)

    def proj(lo, width):
        return _dot(h, w_in_ref[:, lo:lo + width])

    ext_ref[V7X_SUBLANES:V7X_SUBLANES + ts, :] = proj(OFF_XR, D_RNN)
    cw = [conv_w_ref[jj:jj + 1, :] for jj in range(CONV_W)]
    cb = conv_b_ref[...]
    taps = [ext_ref[V7X_SUBLANES - CONV_W + 1 + jj:V7X_SUBLANES - CONV_W + 2 + jj, :]
            for jj in range(CONV_W - 1)]
    qkv_t = []
    rows_per_group = ts // CONV_GROUPS
    tiles_per_group = QKV_TILES // CONV_GROUPS
    for grp in range(CONV_GROUPS):
        for t in range(grp * rows_per_group, (grp + 1) * rows_per_group):
            taps.append(ext_ref[V7X_SUBLANES + t:V7X_SUBLANES + t + 1, :])
            acc = cb + taps[0] * cw[0]
            for jj in range(1, CONV_W):
                acc = acc + taps[jj] * cw[jj]
            xc_ref[t:t + 1, :] = acc
            taps = taps[1:]
        for tile in range(grp * tiles_per_group, (grp + 1) * tiles_per_group):
            qkv_t.append(proj(OFF_Q + tile * V7X_MXU_DIM, V7X_MXU_DIM).T)
    ext_ref[0:V7X_SUBLANES, :] = ext_ref[ts:ts + V7X_SUBLANES, :]

    lam = lam_ref[...]
    half_c_sp = (0.5 * LRU_C) * (jnp.maximum(-lam, 0.0) + jnp.log1p(jnp.exp(-jnp.abs(lam))))
    q_tabs = [rope_ref[t] for t in range(4)]
    k_tabs = [rope_ref[4 + t] for t in range(4)]
    k_tile = qkv_t[Q_W // V7X_MXU_DIM]
    gelu_parts, ta_parts, tb_parts, q_parts, k_parts = [], [], [], [], []
    for c in range(GATE_TILES):
        sl = slice(c * V7X_MXU_DIM, (c + 1) * V7X_MXU_DIM)
        xh = xc_ref[:, sl]
        gz = _dot(xh.astype(BF16), wgate_ref[c])
        t_r = jnp.tanh(gz[:, :V7X_MXU_DIM] + hb_rg_ref[:, sl])
        t_i = jnp.tanh(gz[:, V7X_MXU_DIM:] + hb_ig_ref[:, sl])
        hcs = half_c_sp[:, sl]
        neg_log_a = hcs + hcs * t_r
        a = jnp.exp2(neg_log_a * (-LOG2E))
        a_ref[:, sl] = a
        y = jnp.tanh(neg_log_a) * (a * a + 1.0)
        b_ref[:, sl] = jnp.where(y > 0.0, y * lax.rsqrt(y), 0.0) * (xh + xh * t_i)

        gr = proj(OFF_GR + c * V7X_MXU_DIM, V7X_MXU_DIM)
        gelu_parts.append((gr, jnp.tanh(gr * (GELU_C0 + GELU_C1 * (gr * gr)))))
        ta_parts.append(jnp.tanh(proj(OFF_GA + c * V7X_MXU_DIM, V7X_MXU_DIM)))
        tb_parts.append(jnp.tanh(proj(OFF_GB + c * V7X_MXU_DIM, V7X_MXU_DIM)))
        q_parts.append(_head_norm_rope_t(qkv_t[c], V7X_MXU_DIM // HEAD_DIM, q_tabs).astype(BF16))
        k_parts.append(_head_norm_rope_t(k_tile[c * HEAD_DIM:(c + 1) * HEAD_DIM], 1, k_tabs))
    q_t = jnp.concatenate(q_parts, axis=0)
    kext_ref[WINDOW:WINDOW + ts, :] = jnp.concatenate(k_parts, axis=0).T.astype(BF16)
    v_t = qkv_t[(Q_W + KV_W) // V7X_MXU_DIM].astype(BF16)
    for g in range(N_KV_HEADS):
        vext_ref[g, 0:HEAD_DIM, WINDOW:WINDOW + ts] = v_t[g * HEAD_DIM:(g + 1) * HEAD_DIM]

    hc = hcar_ref[...]
    for t in range(ts):
        hc = a_ref[t:t + 1, :] * hc + b_ref[t:t + 1, :]
        b_ref[t:t + 1, :] = hc
    hcar_ref[...] = hc
    ya_in = []
    for c in range(GATE_TILES):
        gr, tg = gelu_parts[c]
        u = b_ref[:, c * V7X_MXU_DIM:(c + 1) * V7X_MXU_DIM] * gr
        ya_in.append((u + u * tg).astype(BF16))
    ya_in = jnp.concatenate(ya_in, axis=1)

    fold_shape = (WINDOW, GQA_GROUP * WINDOW)
    old_key = (lax.broadcasted_iota(jnp.int32, fold_shape, 0)
               > (lax.broadcasted_iota(jnp.int32, fold_shape, 1) & (WINDOW - 1)))
    no_history = jnp.where(j > 0, 0.0, -jnp.inf).astype(F32)

    def scores(n, g):
        cols = slice(n * WINDOW, (n + 1) * WINDOW)
        q_g = jnp.concatenate([q_t[hd * HEAD_DIM:(hd + 1) * HEAD_DIM, cols]
                               for hd in range(g * GQA_GROUP, (g + 1) * GQA_GROUP)], axis=1)
        return _dot(kext_ref[n * WINDOW:(n + 2) * WINDOW, g * HEAD_DIM:(g + 1) * HEAD_DIM], q_g)

    def softmax(n, g, s_t):
        s_old, s_new = s_t[0:WINDOW], s_t[WINDOW:2 * WINDOW]
        if n == 0:
            s_old = s_old + no_history
        s_win = jnp.where(old_key, s_old, s_new)
        probs, ms = [], []
        for i in range(GQA_GROUP):
            s_i = s_win[:, i * WINDOW:(i + 1) * WINDOW]
            m = jnp.maximum(jnp.max(s_i, axis=0, keepdims=True), sinks2_ref[g * GQA_GROUP + i])
            ms.append(m)
            probs.append(jnp.exp2(s_i - m).astype(BF16))
        p_win = jnp.concatenate(probs, axis=1)
        return jnp.concatenate([p_win * fold_ref[0], p_win * fold_ref[1]], axis=0), ms

    def weighted_values(n, g, p_band, ms):
        cols = slice(n * WINDOW, (n + 1) * WINDOW)
        o_t = _dot(vext_ref[g, :, n * WINDOW:(n + 2) * WINDOW], p_band)
        for i in range(GQA_GROUP):
            hd = g * GQA_GROUP + i
            blk = slice(i * WINDOW, (i + 1) * WINDOW)
            den = o_t[HEAD_DIM:HEAD_DIM + 1, blk] + jnp.exp2(sinks2_ref[hd] - ms[i])
            ao_ref[hd * HEAD_DIM:(hd + 1) * HEAD_DIM, cols] = o_t[0:HEAD_DIM, blk] / den

    work = [(n, g) for n in range(ts // WINDOW) for g in range(N_KV_HEADS)]
    s_q, p_q = {}, {}
    for step in range(len(work) + 2):
        if step < len(work):
            s_q[step] = scores(*work[step])
        if 0 <= step - 1 < len(work):
            p_q[step - 1] = softmax(*work[step - 1], s_q.pop(step - 1))
        if 0 <= step - 2 < len(work):
            weighted_values(*work[step - 2], *p_q.pop(step - 2))
    kext_ref[0:WINDOW, :] = kext_ref[ts:ts + WINDOW, :]
    vext_ref[:, 0:HEAD_DIM, 0:WINDOW] = vext_ref[:, 0:HEAD_DIM, ts:ts + WINDOW]

    y_a = _dot(ya_in, w_rnn_ref[...])
    y_b = _dot_tn(ao_ref[...].astype(BF16), w_attn_ref[...])
    t_a = jnp.concatenate(ta_parts, axis=1)
    t_b = jnp.concatenate(tb_parts, axis=1)
    merged = ((y_a + t_a * y_a) + (y_b + t_b * y_b)).astype(BF16)
    o_ref[0] = x_ref[0] + _dot(merged, w_out_ref[...])


def _ffn_kernel(x_ref, p_ref, g_mlp_ref, w_up_ref, w_down_ref, g_ple_ref, w_pg_ref, w_pp_ref, o_ref):
    x = x_ref[...]
    hm = _rmsnorm(x, g_mlp_ref[...]).astype(BF16)
    acc = x
    for c in range(D_FF // FF_CHUNK):
        u = jnp.maximum(_dot(hm, w_up_ref[:, c * FF_CHUNK:(c + 1) * FF_CHUNK]), 0.0)
        acc = acc + _dot((u * u).astype(BF16), w_down_ref[c * FF_CHUNK:(c + 1) * FF_CHUNK, :])
    hp = _rmsnorm(acc, g_ple_ref[...]).astype(BF16)
    pb = p_ref[...].astype(BF16)
    for c in range(D_MODEL // V7X_MXU_DIM):
        sl = slice(c * V7X_MXU_DIM, (c + 1) * V7X_MXU_DIM)
        t_g = jnp.tanh(_dot(hp, w_pg_ref[:, sl]))
        e_h = _dot(pb, w_pp_ref[:, sl])
        o_ref[:, sl] = acc[:, sl] + (e_h + e_h * t_g)


def _resident(shape):
    nd = len(shape)
    return pl.BlockSpec(shape, lambda *_: (0,) * nd, pipeline_mode=pl.Buffered(1))


def _block_diag_tiles(w):
    per = V7X_MXU_DIM // RNN_BLOCK_W
    w4 = w.reshape(GATE_TILES, per, RNN_BLOCK_W, RNN_BLOCK_W)
    rows = [jnp.pad(w4[:, g], ((0, 0), (0, 0), (g * RNN_BLOCK_W, (per - 1 - g) * RNN_BLOCK_W)))
            for g in range(per)]
    return jnp.concatenate(rows, axis=1)


def _rope_gain_tables(seq, gain, scale):
    inv = ROPE_THETA ** (-np.arange(0, HEAD_DIM, 2, dtype=np.float64) / HEAD_DIM)
    ang = inv[:, None] * np.arange(seq, dtype=np.float64)[None, :]
    cos, sin = jnp.asarray(np.cos(ang), F32), jnp.asarray(np.sin(ang), F32)
    g_lo = (gain[:HALF_DIM] * scale)[:, None]
    g_hi = (gain[HALF_DIM:] * scale)[:, None]
    return [g_lo * cos, g_hi * sin, g_hi * cos, g_lo * sin]


def _fold_masks():
    older = np.arange(WINDOW)[:, None] > np.arange(WINDOW)[None, :]
    both = np.stack([older, ~older]).astype(np.float32)
    return jnp.asarray(np.tile(both, (1, 1, GQA_GROUP)), BF16)


def _mixer(x, sinks2, mats, vecs):
    batch, seq, _ = x.shape
    ts = SEQ_CHUNK
    assert seq % ts == 0 and ts % (CONV_GROUPS * WINDOW) == 0 and QKV_TILES % CONV_GROUPS == 0
    assert GATE_TILES == N_KV_HEADS == Q_W // V7X_MXU_DIM
    (w_in, wgate, w_rnn, w_attn, w_out) = mats
    (g_mix, conv_w, conv_b, hb_rg, hb_ig, lam, rope, fold) = vecs
    in_specs = [
        pl.BlockSpec((1, ts, D_MODEL), lambda b, j, *_: (b, j, 0)),
        _resident(w_in.shape), _resident(g_mix.shape), _resident(conv_w.shape),
        _resident(conv_b.shape), _resident(wgate.shape), _resident(hb_rg.shape), _resident(hb_ig.shape),
        _resident(lam.shape), _resident(w_rnn.shape),
        pl.BlockSpec((rope.shape[0], HALF_DIM, ts), lambda b, j, *_: (0, 0, j)),
        _resident(fold.shape), _resident(w_attn.shape), _resident(w_out.shape),
    ]
    grid_spec = pltpu.PrefetchScalarGridSpec(
        num_scalar_prefetch=1,
        grid=(batch, seq // ts),
        in_specs=in_specs,
        out_specs=pl.BlockSpec((1, ts, D_MODEL), lambda b, j, *_: (b, j, 0)),
        scratch_shapes=[
            pltpu.VMEM((ts + V7X_SUBLANES, D_RNN), F32),
            pltpu.VMEM((ts, D_RNN), F32),
            pltpu.VMEM((ts, D_RNN), F32),
            pltpu.VMEM((ts, D_RNN), F32),
            pltpu.VMEM((1, D_RNN), F32),
            pltpu.VMEM((WINDOW + ts, KV_W), BF16),
            pltpu.VMEM((N_KV_HEADS, V_ROWS, WINDOW + ts), BF16),
            pltpu.VMEM((Q_W, ts), F32),
        ],
    )
    return pl.pallas_call(
        _mixer_kernel,
        grid_spec=grid_spec,
        out_shape=jax.ShapeDtypeStruct(x.shape, F32),
        compiler_params=pltpu.CompilerParams(
            dimension_semantics=("arbitrary", "arbitrary"), vmem_limit_bytes=VMEM_LIMIT),
        name="mixer",
    )(sinks2, x, w_in, g_mix, conv_w, conv_b, wgate, hb_rg, hb_ig, lam, w_rnn, rope, fold,
      w_attn, w_out)


def _ffn(x2d, p2d, g_mlp, w_up, w_down, g_ple, w_pg, w_pp):
    rows = x2d.shape[0]
    tm = FFN_ROWS
    assert rows % tm == 0
    return pl.pallas_call(
        _ffn_kernel,
        grid=(rows // tm,),
        in_specs=[
            pl.BlockSpec((tm, D_MODEL), lambda i: (i, 0)),
            pl.BlockSpec((tm, PLE_DIM), lambda i: (i, 0)),
            _resident(g_mlp.shape), _resident(w_up.shape), _resident(w_down.shape),
            _resident(g_ple.shape), _resident(w_pg.shape), _resident(w_pp.shape),
        ],
        out_specs=pl.BlockSpec((tm, D_MODEL), lambda i: (i, 0)),
        out_shape=jax.ShapeDtypeStruct(x2d.shape, F32),
        compiler_params=pltpu.CompilerParams(
            dimension_semantics=("arbitrary",), vmem_limit_bytes=VMEM_LIMIT),
        name="ffn",
    )(x2d, p2d, g_mlp, w_up, w_down, g_ple, w_pg, w_pp)


def kernel(x, p, g_mix, w_in, conv_w, conv_b, w_rg, b_rg, w_ig, b_ig, lru_lambda, w_rnn_proj, q_gain, k_gain, sinks, w_attn_proj, w_out, g_mlp, w_up, w_down, g_ple, w_ple_gate, w_ple_proj):
    batch, seq, _ = x.shape
    depth = p.shape[0]
    fold = _fold_masks()
    in_scale = jnp.asarray(np.where(np.arange(IN_TOTAL) >= OFF_GA, 0.5, 1.0), F32)
    row = lambda v: v.reshape(1, -1)
    for l in range(depth):
        rope = jnp.stack(_rope_gain_tables(seq, q_gain[l], LOG2E * HEAD_DIM ** -0.5)
                         + _rope_gain_tables(seq, k_gain[l], 1.0))
        w_in_b = (w_in[l] * in_scale).astype(BF16)
        wgate = jnp.concatenate([_block_diag_tiles(w_rg[l]), _block_diag_tiles(w_ig[l])], axis=2).astype(BF16)
        mats = (w_in_b, wgate, (0.5 * w_rnn_proj[l]).astype(BF16),
                w_attn_proj[l].astype(BF16), (0.5 * w_out[l]).astype(BF16))
        vecs = (row(g_mix[l]), 0.5 * conv_w[l], row(0.5 * conv_b[l]), row(0.5 * b_rg[l]), row(0.5 * b_ig[l]),
                row(lru_lambda[l]), rope, fold)
        x = _mixer(x, sinks[l] * LOG2E, mats, vecs)
        x = _ffn(x.reshape(batch * seq, D_MODEL), p[l].reshape(batch * seq, PLE_DIM), row(g_mlp[l]),
                 w_up[l].astype(BF16), w_down[l].astype(BF16), row(g_ple[l]),
                 (0.5 * w_ple_gate[l]).astype(BF16),
                 (0.5 * w_ple_proj[l]).astype(BF16)).reshape(batch, seq, D_MODEL)
    return x
```

```python
import math

import jax
import jax.numpy as jnp
import numpy as np
from jax import lax
from jax.experimental import pallas as pl
from jax.experimental.pallas import tpu as pltpu

D_MODEL = 1024
D_RNN = D_MODEL
RNN_BLOCKS = 16
RNN_BLOCK_W = D_RNN // RNN_BLOCKS
CONV_W = 4
LRU_C = 8.0
HEAD_DIM = 64
HALF_DIM = HEAD_DIM // 2
N_Q_HEADS = D_MODEL // HEAD_DIM
N_KV_HEADS = 4
GQA_GROUP = N_Q_HEADS // N_KV_HEADS
WINDOW = 128
ROPE_THETA = 10000.0
Q_W = N_Q_HEADS * HEAD_DIM
KV_W = N_KV_HEADS * HEAD_DIM
QKV_W = Q_W + 2 * KV_W
D_FF = 4 * D_MODEL
PLE_DIM = 256
NORM_EPS = 1e-6
LOG2E = math.log2(math.e)
GELU_C0 = math.sqrt(2.0 / math.pi)
GELU_C1 = GELU_C0 * 0.044715

OFF_XR = 0
OFF_GR = OFF_XR + D_RNN
OFF_Q = OFF_GR + D_RNN
OFF_GA = OFF_Q + QKV_W
OFF_GB = OFF_GA + D_MODEL
IN_TOTAL = OFF_GB + D_MODEL

V7X_LANES = 128
V7X_SUBLANES = 8
V7X_BF16_SUBLANES = 16
V7X_MXU_DIM = 256
V7X_VMEM_BYTES = 64 * 1024 * 1024

GATE_TILES = D_RNN // V7X_MXU_DIM
QKV_TILES = QKV_W // V7X_MXU_DIM
SEQ_CHUNK = 512
CONV_GROUPS = 2
FFN_ROWS = 1024
FF_CHUNK = 1024
V_ROWS = HEAD_DIM + V7X_BF16_SUBLANES
VMEM_LIMIT = V7X_VMEM_BYTES * 3 // 4

BF16 = jnp.bfloat16
F32 = jnp.float32


def _dot(a, b):
    return jnp.dot(a, b, preferred_element_type=F32)


def _dot_tn(a, b):
    return lax.dot_general(a, b, (((0,), (0,)), ((), ())), preferred_element_type=F32)


def _rmsnorm(t, g):
    ms = jnp.mean(t * t, axis=-1, keepdims=True)
    return t * lax.rsqrt(ms + NORM_EPS) * g


def _head_norm_rope_t(z_t, n_heads, tabs):
    c_lo, s_hi, c_hi, s_lo = tabs
    outs = []
    for hd in range(n_heads):
        t = z_t[hd * HEAD_DIM:(hd + 1) * HEAD_DIM]
        inv = lax.rsqrt(jnp.sum(t * t, axis=0, keepdims=True) * (1.0 / HEAD_DIM) + NORM_EPS)
        t1, t2 = t[:HALF_DIM], t[HALF_DIM:]
        outs.append((t1 * c_lo - t2 * s_hi) * inv)
        outs.append((t2 * c_hi + t1 * s_lo) * inv)
    return jnp.concatenate(outs, axis=0)


def _mixer_kernel(sinks2_ref, x_ref, w_in_ref, g_mix_ref, conv_w_ref, conv_b_ref, wgate_ref,
                  hb_rg_ref, hb_ig_ref, lam_ref, w_rnn_ref, rope_ref, fold_ref, w_attn_ref, w_out_ref,
                  o_ref,
                  ext_ref, xc_ref, a_ref, b_ref, hcar_ref, kext_ref, vext_ref, ao_ref):
    ts = SEQ_CHUNK
    j = pl.program_id(1)

    @pl.when(j == 0)
    def _():
        ext_ref[0:V7X_SUBLANES, :] = jnp.zeros((V7X_SUBLANES, D_RNN), F32)
        hcar_ref[...] = jnp.zeros_like(hcar_ref)
        kext_ref[0:WINDOW, :] = jnp.zeros((WINDOW, KV_W), BF16)
        vext_ref[:, 0:HEAD_DIM, 0:WINDOW] = jnp.zeros((N_KV_HEADS, HEAD_DIM, WINDOW), BF16)
        vext_ref[:, HEAD_DIM:V_ROWS, :] = jnp.ones((N_KV_HEADS, V_ROWS - HEAD_DIM, WINDOW + ts), BF16)

    h = _rmsnorm(x_ref[0], g_mix_ref[...]).astype(BF16)

    def proj(lo, width):
        return _dot(h, w_in_ref[:, lo:lo + width])

    ext_ref[V7X_SUBLANES:V7X_SUBLANES + ts, :] = proj(OFF_XR, D_RNN)
    cw = [conv_w_ref[jj:jj + 1, :] for jj in range(CONV_W)]
    cb = conv_b_ref[...]
    taps = [ext_ref[V7X_SUBLANES - CONV_W + 1 + jj:V7X_SUBLANES - CONV_W + 2 + jj, :]
            for jj in range(CONV_W - 1)]
    qkv_t = []
    rows_per_group = ts // CONV_GROUPS
    tiles_per_group = QKV_TILES // CONV_GROUPS
    for grp in range(CONV_GROUPS):
        for t in range(grp * rows_per_group, (grp + 1) * rows_per_group):
            taps.append(ext_ref[V7X_SUBLANES + t:V7X_SUBLANES + t + 1, :])
            acc = cb + taps[0] * cw[0]
            for jj in range(1, CONV_W):
                acc = acc + taps[jj] * cw[jj]
            xc_ref[t:t + 1, :] = acc
            taps = taps[1:]
        for tile in range(grp * tiles_per_group, (grp + 1) * tiles_per_group):
            qkv_t.append(proj(OFF_Q + tile * V7X_MXU_DIM, V7X_MXU_DIM).T)
    ext_ref[0:V7X_SUBLANES, :] = ext_ref[ts:ts + V7X_SUBLANES, :]

    lam = lam_ref[...]
    half_c_sp = (0.5 * LRU_C) * (jnp.maximum(-lam, 0.0) + jnp.log1p(jnp.exp(-jnp.abs(lam))))
    pos = pl.ds(pl.multiple_of(j * ts, ts), ts)
    q_tabs = [rope_ref[t, :, pos] for t in range(4)]
    k_tabs = [rope_ref[4 + t, :, pos] for t in range(4)]
    k_tile = qkv_t[Q_W // V7X_MXU_DIM]
    gelu_parts, ta_parts, tb_parts, q_parts, k_parts = [], [], [], [], []
    for c in range(GATE_TILES):
        sl = slice(c * V7X_MXU_DIM, (c + 1) * V7X_MXU_DIM)
        xh = xc_ref[:, sl]
        gz = _dot(xh.astype(BF16), wgate_ref[c])
        t_r = jnp.tanh(gz[:, :V7X_MXU_DIM] + hb_rg_ref[:, sl])
        t_i = jnp.tanh(gz[:, V7X_MXU_DIM:] + hb_ig_ref[:, sl])
        hcs = half_c_sp[:, sl]
        neg_log_a = hcs + hcs * t_r
        a = jnp.exp2(neg_log_a * (-LOG2E))
        a_ref[:, sl] = a
        y = jnp.tanh(neg_log_a) * (a * a + 1.0)
        b_ref[:, sl] = jnp.where(y > 0.0, y * lax.rsqrt(y), 0.0) * (xh + xh * t_i)

        gr = proj(OFF_GR + c * V7X_MXU_DIM, V7X_MXU_DIM)
        gelu_parts.append((gr, jnp.tanh(gr * (GELU_C0 + GELU_C1 * (gr * gr)))))
        ta_parts.append(jnp.tanh(proj(OFF_GA + c * V7X_MXU_DIM, V7X_MXU_DIM)))
        tb_parts.append(jnp.tanh(proj(OFF_GB + c * V7X_MXU_DIM, V7X_MXU_DIM)))
        q_parts.append(_head_norm_rope_t(qkv_t[c], V7X_MXU_DIM // HEAD_DIM, q_tabs).astype(BF16))
        k_parts.append(_head_norm_rope_t(k_tile[c * HEAD_DIM:(c + 1) * HEAD_DIM], 1, k_tabs))
    q_t = jnp.concatenate(q_parts, axis=0)
    kext_ref[WINDOW:WINDOW + ts, :] = jnp.concatenate(k_parts, axis=0).T.astype(BF16)
    v_t = qkv_t[(Q_W + KV_W) // V7X_MXU_DIM].astype(BF16)
    for g in range(N_KV_HEADS):
        vext_ref[g, 0:HEAD_DIM, WINDOW:WINDOW + ts] = v_t[g * HEAD_DIM:(g + 1) * HEAD_DIM]

    hc = hcar_ref[...]
    for t in range(ts):
        hc = a_ref[t:t + 1, :] * hc + b_ref[t:t + 1, :]
        b_ref[t:t + 1, :] = hc
    hcar_ref[...] = hc
    ya_in = []
    for c in range(GATE_TILES):
        gr, tg = gelu_parts[c]
        u = b_ref[:, c * V7X_MXU_DIM:(c + 1) * V7X_MXU_DIM] * gr
        ya_in.append((u + u * tg).astype(BF16))
    ya_in = jnp.concatenate(ya_in, axis=1)

    fold_shape = (WINDOW, GQA_GROUP * WINDOW)
    old_key = (lax.broadcasted_iota(jnp.int32, fold_shape, 0)
               > (lax.broadcasted_iota(jnp.int32, fold_shape, 1) & (WINDOW - 1)))
    no_history = jnp.where(j > 0, 0.0, -jnp.inf).astype(F32)

    def scores(n, g):
        cols = slice(n * WINDOW, (n + 1) * WINDOW)
        q_g = jnp.concatenate([q_t[hd * HEAD_DIM:(hd + 1) * HEAD_DIM, cols]
                               for hd in range(g * GQA_GROUP, (g + 1) * GQA_GROUP)], axis=1)
        return _dot(kext_ref[n * WINDOW:(n + 2) * WINDOW, g * HEAD_DIM:(g + 1) * HEAD_DIM], q_g)

    def softmax(n, g, s_t):
        s_old, s_new = s_t[0:WINDOW], s_t[WINDOW:2 * WINDOW]
        if n == 0:
            s_old = s_old + no_history
        s_win = jnp.where(old_key, s_old, s_new)
        probs, ms = [], []
        for i in range(GQA_GROUP):
            s_i = s_win[:, i * WINDOW:(i + 1) * WINDOW]
            m = jnp.maximum(jnp.max(s_i, axis=0, keepdims=True), sinks2_ref[g * GQA_GROUP + i])
            ms.append(m)
            probs.append(jnp.exp2(s_i - m).astype(BF16))
        p_win = jnp.concatenate(probs, axis=1)
        return jnp.concatenate([p_win * fold_ref[0], p_win * fold_ref[1]], axis=0), ms

    def weighted_values(n, g, p_band, ms):
        cols = slice(n * WINDOW, (n + 1) * WINDOW)
        o_t = _dot(vext_ref[g, :, n * WINDOW:(n + 2) * WINDOW], p_band)
        for i in range(GQA_GROUP):
            hd = g * GQA_GROUP + i
            blk = slice(i * WINDOW, (i + 1) * WINDOW)
            den = o_t[HEAD_DIM:HEAD_DIM + 1, blk] + jnp.exp2(sinks2_ref[hd] - ms[i])
            ao_ref[hd * HEAD_DIM:(hd + 1) * HEAD_DIM, cols] = o_t[0:HEAD_DIM, blk] / den

    work = [(n, g) for n in range(ts // WINDOW) for g in range(N_KV_HEADS)]
    s_q, p_q = {}, {}
    for step in range(len(work) + 2):
        if step < len(work):
            s_q[step] = scores(*work[step])
        if 0 <= step - 1 < len(work):
            p_q[step - 1] = softmax(*work[step - 1], s_q.pop(step - 1))
        if 0 <= step - 2 < len(work):
            weighted_values(*work[step - 2], *p_q.pop(step - 2))
    kext_ref[0:WINDOW, :] = kext_ref[ts:ts + WINDOW, :]
    vext_ref[:, 0:HEAD_DIM, 0:WINDOW] = vext_ref[:, 0:HEAD_DIM, ts:ts + WINDOW]

    y_a = _dot(ya_in, w_rnn_ref[...])
    y_b = _dot_tn(ao_ref[...].astype(BF16), w_attn_ref[...])
    t_a = jnp.concatenate(ta_parts, axis=1)
    t_b = jnp.concatenate(tb_parts, axis=1)
    merged = ((y_a + t_a * y_a) + (y_b + t_b * y_b)).astype(BF16)
    o_ref[0] = x_ref[0] + _dot(merged, w_out_ref[...])


def _ffn_kernel(x_ref, p_ref, g_mlp_ref, w_up_ref, w_down_ref, g_ple_ref, w_pg_ref, w_pp_ref, o_ref):
    x = x_ref[...]
    hm = _rmsnorm(x, g_mlp_ref[...]).astype(BF16)
    acc = x
    for c in range(D_FF // FF_CHUNK):
        u = jnp.maximum(_dot(hm, w_up_ref[:, c * FF_CHUNK:(c + 1) * FF_CHUNK]), 0.0)
        acc = acc + _dot((u * u).astype(BF16), w_down_ref[c * FF_CHUNK:(c + 1) * FF_CHUNK, :])
    hp = _rmsnorm(acc, g_ple_ref[...]).astype(BF16)
    pb = p_ref[...].astype(BF16)
    for c in range(D_MODEL // V7X_MXU_DIM):
        sl = slice(c * V7X_MXU_DIM, (c + 1) * V7X_MXU_DIM)
        t_g = jnp.tanh(_dot(hp, w_pg_ref[:, sl]))
        e_h = _dot(pb, w_pp_ref[:, sl])
        o_ref[:, sl] = acc[:, sl] + (e_h + e_h * t_g)


def _resident(shape):
    nd = len(shape)
    return pl.BlockSpec(shape, lambda *_: (0,) * nd, pipeline_mode=pl.Buffered(1))


def _block_diag_tiles(w):
    per = V7X_MXU_DIM // RNN_BLOCK_W
    w4 = w.reshape(GATE_TILES, per, RNN_BLOCK_W, RNN_BLOCK_W)
    rows = [jnp.pad(w4[:, g], ((0, 0), (0, 0), (g * RNN_BLOCK_W, (per - 1 - g) * RNN_BLOCK_W)))
            for g in range(per)]
    return jnp.concatenate(rows, axis=1)


def _rope_gain_tables(seq, gain, scale):
    inv = ROPE_THETA ** (-np.arange(0, HEAD_DIM, 2, dtype=np.float64) / HEAD_DIM)
    ang = inv[:, None] * np.arange(seq, dtype=np.float64)[None, :]
    cos, sin = jnp.asarray(np.cos(ang), F32), jnp.asarray(np.sin(ang), F32)
    g_lo = (gain[:HALF_DIM] * scale)[:, None]
    g_hi = (gain[HALF_DIM:] * scale)[:, None]
    return [g_lo * cos, g_hi * sin, g_hi * cos, g_lo * sin]


def _fold_masks():
    older = np.arange(WINDOW)[:, None] > np.arange(WINDOW)[None, :]
    both = np.stack([older, ~older]).astype(np.float32)
    return jnp.asarray(np.tile(both, (1, 1, GQA_GROUP)), BF16)


def _mixer(x, sinks2, mats, vecs):
    batch, seq, _ = x.shape
    ts = SEQ_CHUNK
    assert seq % ts == 0 and ts % (CONV_GROUPS * WINDOW) == 0 and QKV_TILES % CONV_GROUPS == 0
    assert GATE_TILES == N_KV_HEADS == Q_W // V7X_MXU_DIM
    (w_in, wgate, w_rnn, w_attn, w_out) = mats
    (g_mix, conv_w, conv_b, hb_rg, hb_ig, lam, rope, fold) = vecs
    in_specs = [
        pl.BlockSpec((1, ts, D_MODEL), lambda b, j, *_: (b, j, 0)),
        _resident(w_in.shape), _resident(g_mix.shape), _resident(conv_w.shape),
        _resident(conv_b.shape), _resident(wgate.shape), _resident(hb_rg.shape), _resident(hb_ig.shape),
        _resident(lam.shape), _resident(w_rnn.shape),
        _resident(rope.shape),
        _resident(fold.shape), _resident(w_attn.shape), _resident(w_out.shape),
    ]
    grid_spec = pltpu.PrefetchScalarGridSpec(
        num_scalar_prefetch=1,
        grid=(batch, seq // ts),
        in_specs=in_specs,
        out_specs=pl.BlockSpec((1, ts, D_MODEL), lambda b, j, *_: (b, j, 0)),
        scratch_shapes=[
            pltpu.VMEM((ts + V7X_SUBLANES, D_RNN), F32),
            pltpu.VMEM((ts, D_RNN), F32),
            pltpu.VMEM((ts, D_RNN), F32),
            pltpu.VMEM((ts, D_RNN), F32),
            pltpu.VMEM((1, D_RNN), F32),
            pltpu.VMEM((WINDOW + ts, KV_W), BF16),
            pltpu.VMEM((N_KV_HEADS, V_ROWS, WINDOW + ts), BF16),
            pltpu.VMEM((Q_W, ts), F32),
        ],
    )
    return pl.pallas_call(
        _mixer_kernel,
        grid_spec=grid_spec,
        out_shape=jax.ShapeDtypeStruct(x.shape, F32),
        compiler_params=pltpu.CompilerParams(
            dimension_semantics=("arbitrary", "arbitrary"), vmem_limit_bytes=VMEM_LIMIT),
        name="mixer",
    )(sinks2, x, w_in, g_mix, conv_w, conv_b, wgate, hb_rg, hb_ig, lam, w_rnn, rope, fold,
      w_attn, w_out)


def _ffn(x2d, p2d, g_mlp, w_up, w_down, g_ple, w_pg, w_pp):
    rows = x2d.shape[0]
    tm = FFN_ROWS
    assert rows % tm == 0
    return pl.pallas_call(
        _ffn_kernel,
        grid=(rows // tm,),
        in_specs=[
            pl.BlockSpec((tm, D_MODEL), lambda i: (i, 0)),
            pl.BlockSpec((tm, PLE_DIM), lambda i: (i, 0)),
            _resident(g_mlp.shape), _resident(w_up.shape), _resident(w_down.shape),
            _resident(g_ple.shape), _resident(w_pg.shape), _resident(w_pp.shape),
        ],
        out_specs=pl.BlockSpec((tm, D_MODEL), lambda i: (i, 0)),
        out_shape=jax.ShapeDtypeStruct(x2d.shape, F32),
        compiler_params=pltpu.CompilerParams(
            dimension_semantics=("arbitrary",), vmem_limit_bytes=VMEM_LIMIT),
        name="ffn",
    )(x2d, p2d, g_mlp, w_up, w_down, g_ple, w_pg, w_pp)


def kernel(x, p, g_mix, w_in, conv_w, conv_b, w_rg, b_rg, w_ig, b_ig, lru_lambda, w_rnn_proj, q_gain, k_gain, sinks, w_attn_proj, w_out, g_mlp, w_up, w_down, g_ple, w_ple_gate, w_ple_proj):
    batch, seq, _ = x.shape
    depth = p.shape[0]
    fold = _fold_masks()
    in_scale = jnp.asarray(np.where(np.arange(IN_TOTAL) >= OFF_GA, 0.5, 1.0), F32)
    row = lambda v: v.reshape(1, -1)
    for l in range(depth):
        rope = jnp.stack(_rope_gain_tables(seq, q_gain[l], LOG2E * HEAD_DIM ** -0.5)
                         + _rope_gain_tables(seq, k_gain[l], 1.0))
        w_in_b = (w_in[l] * in_scale).astype(BF16)
        wgate = jnp.concatenate([_block_diag_tiles(w_rg[l]), _block_diag_tiles(w_ig[l])], axis=2).astype(BF16)
        mats = (w_in_b, wgate, (0.5 * w_rnn_proj[l]).astype(BF16),
                w_attn_proj[l].astype(BF16), (0.5 * w_out[l]).astype(BF16))
        vecs = (row(g_mix[l]), 0.5 * conv_w[l], row(0.5 * conv_b[l]), row(0.5 * b_rg[l]), row(0.5 * b_ig[l]),
                row(lru_lambda[l]), rope, fold)
        x = _mixer(x, sinks[l] * LOG2E, mats, vecs)
        x = _ffn(x.reshape(batch * seq, D_MODEL), p[l].reshape(batch * seq, PLE_DIM), row(g_mlp[l]),
                 w_up[l].astype(BF16), w_down[l].astype(BF16), row(g_ple[l]),
                 (0.5 * w_ple_gate[l]).astype(BF16),
                 (0.5 * w_ple_proj[l]).astype(BF16)).reshape(batch, seq, D_MODEL)
    return x
```

```python
import math

import jax
import jax.numpy as jnp
import numpy as np
from jax import lax
from jax.experimental import pallas as pl
from jax.experimental.pallas import tpu as pltpu

D_MODEL = 1024
D_RNN = D_MODEL
RNN_BLOCKS = 16
RNN_BLOCK_W = D_RNN // RNN_BLOCKS
CONV_W = 4
LRU_C = 8.0
HEAD_DIM = 64
HALF_DIM = HEAD_DIM // 2
N_Q_HEADS = D_MODEL // HEAD_DIM
N_KV_HEADS = 4
GQA_GROUP = N_Q_HEADS // N_KV_HEADS
WINDOW = 128
ROPE_THETA = 10000.0
Q_W = N_Q_HEADS * HEAD_DIM
KV_W = N_KV_HEADS * HEAD_DIM
QKV_W = Q_W + 2 * KV_W
D_FF = 4 * D_MODEL
PLE_DIM = 256
NORM_EPS = 1e-6
LOG2E = math.log2(math.e)
GELU_C0 = math.sqrt(2.0 / math.pi)
GELU_C1 = GELU_C0 * 0.044715

OFF_XR = 0
OFF_GR = OFF_XR + D_RNN
OFF_Q = OFF_GR + D_RNN
OFF_GA = OFF_Q + QKV_W
OFF_GB = OFF_GA + D_MODEL
IN_TOTAL = OFF_GB + D_MODEL

V7X_LANES = 128
V7X_SUBLANES = 8
V7X_BF16_SUBLANES = 16
V7X_MXU_DIM = 256
V7X_VMEM_BYTES = 64 * 1024 * 1024

GATE_TILES = D_RNN // V7X_MXU_DIM
QKV_TILES = QKV_W // V7X_MXU_DIM
SEQ_CHUNK = 512
FFN_ROWS = 1024
FF_CHUNK = 1024
V_ROWS = HEAD_DIM + V7X_BF16_SUBLANES
VMEM_LIMIT = V7X_VMEM_BYTES * 3 // 4

BF16 = jnp.bfloat16
F32 = jnp.float32


def _dot(a, b):
    return jnp.dot(a, b, preferred_element_type=F32)


def _dot_tn(a, b):
    return lax.dot_general(a, b, (((0,), (0,)), ((), ())), preferred_element_type=F32)


def _rmsnorm(t, g):
    ms = jnp.mean(t * t, axis=-1, keepdims=True)
    return t * lax.rsqrt(ms + NORM_EPS) * g


def _head_norm_rope_t(z_t, n_heads, tabs):
    c_lo, s_hi, c_hi, s_lo = tabs
    outs = []
    for hd in range(n_heads):
        t = z_t[hd * HEAD_DIM:(hd + 1) * HEAD_DIM]
        inv = lax.rsqrt(jnp.sum(t * t, axis=0, keepdims=True) * (1.0 / HEAD_DIM) + NORM_EPS)
        t1, t2 = t[:HALF_DIM], t[HALF_DIM:]
        outs.append((t1 * c_lo - t2 * s_hi) * inv)
        outs.append((t2 * c_hi + t1 * s_lo) * inv)
    return jnp.concatenate(outs, axis=0)


def _mixer_kernel(sinks2_ref, x_ref, w_in_ref, g_mix_ref, conv_w_ref, conv_b_ref, wgate_ref,
                  hb_rg_ref, hb_ig_ref, lam_ref, w_rnn_ref, rope_ref, fold_ref, w_attn_ref, w_out_ref,
                  o_ref,
                  ext_ref, xc_ref, a_ref, b_ref, hcar_ref, kext_ref, vext_ref, ao_ref):
    ts = SEQ_CHUNK
    j = pl.program_id(1)

    @pl.when(j == 0)
    def _():
        ext_ref[0:V7X_SUBLANES, :] = jnp.zeros((V7X_SUBLANES, D_RNN), F32)
        hcar_ref[...] = jnp.zeros_like(hcar_ref)
        kext_ref[0:WINDOW, :] = jnp.zeros((WINDOW, KV_W), BF16)
        vext_ref[:, 0:HEAD_DIM, 0:WINDOW] = jnp.zeros((N_KV_HEADS, HEAD_DIM, WINDOW), BF16)
        vext_ref[:, HEAD_DIM:V_ROWS, :] = jnp.ones((N_KV_HEADS, V_ROWS - HEAD_DIM, WINDOW + ts), BF16)

    h = _rmsnorm(x_ref[0], g_mix_ref[...]).astype(BF16)

    def proj(lo, width):
        return _dot(h, w_in_ref[:, lo:lo + width])

    ext_ref[V7X_SUBLANES:V7X_SUBLANES + ts, :] = proj(OFF_XR, D_RNN)
    cw = [conv_w_ref[jj:jj + 1, :] for jj in range(CONV_W)]
    cb = conv_b_ref[...]
    taps = [ext_ref[V7X_SUBLANES - CONV_W + 1 + jj:V7X_SUBLANES - CONV_W + 2 + jj, :]
            for jj in range(CONV_W - 1)]
    for t in range(ts):
        taps.append(ext_ref[V7X_SUBLANES + t:V7X_SUBLANES + t + 1, :])
        acc = cb + taps[0] * cw[0]
        for jj in range(1, CONV_W):
            acc = acc + taps[jj] * cw[jj]
        xc_ref[t:t + 1, :] = acc
        taps = taps[1:]
    ext_ref[0:V7X_SUBLANES, :] = ext_ref[ts:ts + V7X_SUBLANES, :]

    qkv_t = [proj(OFF_Q + tile * V7X_MXU_DIM, V7X_MXU_DIM).T for tile in range(QKV_TILES)]

    lam = lam_ref[...]
    half_c_sp = (0.5 * LRU_C) * (jnp.maximum(-lam, 0.0) + jnp.log1p(jnp.exp(-jnp.abs(lam))))
    pos = pl.ds(pl.multiple_of(j * ts, ts), ts)
    q_tabs = [rope_ref[t, :, pos] for t in range(4)]
    k_tabs = [rope_ref[4 + t, :, pos] for t in range(4)]
    k_tile = qkv_t[Q_W // V7X_MXU_DIM]
    gelu_parts, ta_parts, tb_parts, q_parts, k_parts = [], [], [], [], []
    for c in range(GATE_TILES):
        sl = slice(c * V7X_MXU_DIM, (c + 1) * V7X_MXU_DIM)
        xh = xc_ref[:, sl]
        gz = _dot(xh.astype(BF16), wgate_ref[c])
        t_r = jnp.tanh(gz[:, :V7X_MXU_DIM] + hb_rg_ref[:, sl])
        t_i = jnp.tanh(gz[:, V7X_MXU_DIM:] + hb_ig_ref[:, sl])
        hcs = half_c_sp[:, sl]
        neg_log_a = hcs + hcs * t_r
        a = jnp.exp2(neg_log_a * (-LOG2E))
        a_ref[:, sl] = a
        y = jnp.tanh(neg_log_a) * (a * a + 1.0)
        b_ref[:, sl] = jnp.where(y > 0.0, y * lax.rsqrt(y), 0.0) * (xh + xh * t_i)

        gr = proj(OFF_GR + c * V7X_MXU_DIM, V7X_MXU_DIM)
        gelu_parts.append((gr, jnp.tanh(gr * (GELU_C0 + GELU_C1 * (gr * gr)))))
        ta_parts.append(jnp.tanh(proj(OFF_GA + c * V7X_MXU_DIM, V7X_MXU_DIM)))
        tb_parts.append(jnp.tanh(proj(OFF_GB + c * V7X_MXU_DIM, V7X_MXU_DIM)))
        q_parts.append(_head_norm_rope_t(qkv_t[c], V7X_MXU_DIM // HEAD_DIM, q_tabs).astype(BF16))
        k_parts.append(_head_norm_rope_t(k_tile[c * HEAD_DIM:(c + 1) * HEAD_DIM], 1, k_tabs))
    q_t = jnp.concatenate(q_parts, axis=0)
    kext_ref[WINDOW:WINDOW + ts, :] = jnp.concatenate(k_parts, axis=0).T.astype(BF16)
    v_t = qkv_t[(Q_W + KV_W) // V7X_MXU_DIM].astype(BF16)
    for g in range(N_KV_HEADS):
        vext_ref[g, 0:HEAD_DIM, WINDOW:WINDOW + ts] = v_t[g * HEAD_DIM:(g + 1) * HEAD_DIM]

    hc = hcar_ref[...]
    for t in range(ts):
        hc = a_ref[t:t + 1, :] * hc + b_ref[t:t + 1, :]
        b_ref[t:t + 1, :] = hc
    hcar_ref[...] = hc
    ya_in = []
    for c in range(GATE_TILES):
        gr, tg = gelu_parts[c]
        u = b_ref[:, c * V7X_MXU_DIM:(c + 1) * V7X_MXU_DIM] * gr
        ya_in.append((u + u * tg).astype(BF16))
    ya_in = jnp.concatenate(ya_in, axis=1)

    fold_shape = (WINDOW, GQA_GROUP * WINDOW)
    old_key = (lax.broadcasted_iota(jnp.int32, fold_shape, 0)
               > (lax.broadcasted_iota(jnp.int32, fold_shape, 1) & (WINDOW - 1)))
    no_history = jnp.where(j > 0, 0.0, -jnp.inf).astype(F32)

    def scores(n, g):
        cols = slice(n * WINDOW, (n + 1) * WINDOW)
        q_g = jnp.concatenate([q_t[hd * HEAD_DIM:(hd + 1) * HEAD_DIM, cols]
                               for hd in range(g * GQA_GROUP, (g + 1) * GQA_GROUP)], axis=1)
        return _dot(kext_ref[n * WINDOW:(n + 2) * WINDOW, g * HEAD_DIM:(g + 1) * HEAD_DIM], q_g)

    def softmax(n, g, s_t):
        s_old, s_new = s_t[0:WINDOW], s_t[WINDOW:2 * WINDOW]
        if n == 0:
            s_old = s_old + no_history
        s_win = jnp.where(old_key, s_old, s_new)
        probs, ms = [], []
        for i in range(GQA_GROUP):
            s_i = s_win[:, i * WINDOW:(i + 1) * WINDOW]
            m = jnp.maximum(jnp.max(s_i, axis=0, keepdims=True), sinks2_ref[g * GQA_GROUP + i])
            ms.append(m)
            probs.append(jnp.exp2(s_i - m).astype(BF16))
        p_win = jnp.concatenate(probs, axis=1)
        return jnp.concatenate([p_win * fold_ref[0], p_win * fold_ref[1]], axis=0), ms

    def weighted_values(n, g, p_band, ms):
        cols = slice(n * WINDOW, (n + 1) * WINDOW)
        o_t = _dot(vext_ref[g, :, n * WINDOW:(n + 2) * WINDOW], p_band)
        for i in range(GQA_GROUP):
            hd = g * GQA_GROUP + i
            blk = slice(i * WINDOW, (i + 1) * WINDOW)
            den = o_t[HEAD_DIM:HEAD_DIM + 1, blk] + jnp.exp2(sinks2_ref[hd] - ms[i])
            ao_ref[hd * HEAD_DIM:(hd + 1) * HEAD_DIM, cols] = o_t[0:HEAD_DIM, blk] / den

    work = [(n, g) for n in range(ts // WINDOW) for g in range(N_KV_HEADS)]
    s_q, p_q = {}, {}
    for step in range(len(work) + 2):
        if step < len(work):
            s_q[step] = scores(*work[step])
        if 0 <= step - 1 < len(work):
            p_q[step - 1] = softmax(*work[step - 1], s_q.pop(step - 1))
        if 0 <= step - 2 < len(work):
            weighted_values(*work[step - 2], *p_q.pop(step - 2))
    kext_ref[0:WINDOW, :] = kext_ref[ts:ts + WINDOW, :]
    vext_ref[:, 0:HEAD_DIM, 0:WINDOW] = vext_ref[:, 0:HEAD_DIM, ts:ts + WINDOW]

    y_a = _dot(ya_in, w_rnn_ref[...])
    y_b = _dot_tn(ao_ref[...].astype(BF16), w_attn_ref[...])
    t_a = jnp.concatenate(ta_parts, axis=1)
    t_b = jnp.concatenate(tb_parts, axis=1)
    merged = ((y_a + t_a * y_a) + (y_b + t_b * y_b)).astype(BF16)
    o_ref[0] = x_ref[0] + _dot(merged, w_out_ref[...])


def _ffn_kernel(x_ref, p_ref, g_mlp_ref, w_up_ref, w_down_ref, g_ple_ref, w_pg_ref, w_pp_ref, o_ref):
    x = x_ref[...]
    hm = _rmsnorm(x, g_mlp_ref[...]).astype(BF16)
    acc = x
    for c in range(D_FF // FF_CHUNK):
        u = jnp.maximum(_dot(hm, w_up_ref[:, c * FF_CHUNK:(c + 1) * FF_CHUNK]), 0.0)
        acc = acc + _dot((u * u).astype(BF16), w_down_ref[c * FF_CHUNK:(c + 1) * FF_CHUNK, :])
    hp = _rmsnorm(acc, g_ple_ref[...]).astype(BF16)
    pb = p_ref[...].astype(BF16)
    for c in range(D_MODEL // V7X_MXU_DIM):
        sl = slice(c * V7X_MXU_DIM, (c + 1) * V7X_MXU_DIM)
        t_g = jnp.tanh(_dot(hp, w_pg_ref[:, sl]))
        e_h = _dot(pb, w_pp_ref[:, sl])
        o_ref[:, sl] = acc[:, sl] + (e_h + e_h * t_g)


def _resident(shape):
    nd = len(shape)
    return pl.BlockSpec(shape, lambda *_: (0,) * nd, pipeline_mode=pl.Buffered(1))


def _block_diag_tiles(w):
    per = V7X_MXU_DIM // RNN_BLOCK_W
    w4 = w.reshape(GATE_TILES, per, RNN_BLOCK_W, RNN_BLOCK_W)
    rows = [jnp.pad(w4[:, g], ((0, 0), (0, 0), (g * RNN_BLOCK_W, (per - 1 - g) * RNN_BLOCK_W)))
            for g in range(per)]
    return jnp.concatenate(rows, axis=1)


def _rope_gain_tables(seq, gain, scale):
    inv = ROPE_THETA ** (-np.arange(0, HEAD_DIM, 2, dtype=np.float64) / HEAD_DIM)
    ang = inv[:, None] * np.arange(seq, dtype=np.float64)[None, :]
    cos, sin = jnp.asarray(np.cos(ang), F32), jnp.asarray(np.sin(ang), F32)
    g_lo = (gain[:HALF_DIM] * scale)[:, None]
    g_hi = (gain[HALF_DIM:] * scale)[:, None]
    return [g_lo * cos, g_hi * sin, g_hi * cos, g_lo * sin]


def _fold_masks():
    older = np.arange(WINDOW)[:, None] > np.arange(WINDOW)[None, :]
    both = np.stack([older, ~older]).astype(np.float32)
    return jnp.asarray(np.tile(both, (1, 1, GQA_GROUP)), BF16)


def _mixer(x, sinks2, mats, vecs):
    batch, seq, _ = x.shape
    ts = SEQ_CHUNK
    assert seq % ts == 0 and ts % WINDOW == 0
    assert GATE_TILES == N_KV_HEADS == Q_W // V7X_MXU_DIM
    (w_in, wgate, w_rnn, w_attn, w_out) = mats
    (g_mix, conv_w, conv_b, hb_rg, hb_ig, lam, rope, fold) = vecs
    in_specs = [
        pl.BlockSpec((1, ts, D_MODEL), lambda b, j, *_: (b, j, 0)),
        _resident(w_in.shape), _resident(g_mix.shape), _resident(conv_w.shape),
        _resident(conv_b.shape), _resident(wgate.shape), _resident(hb_rg.shape), _resident(hb_ig.shape),
        _resident(lam.shape), _resident(w_rnn.shape),
        _resident(rope.shape),
        _resident(fold.shape), _resident(w_attn.shape), _resident(w_out.shape),
    ]
    grid_spec = pltpu.PrefetchScalarGridSpec(
        num_scalar_prefetch=1,
        grid=(batch, seq // ts),
        in_specs=in_specs,
        out_specs=pl.BlockSpec((1, ts, D_MODEL), lambda b, j, *_: (b, j, 0)),
        scratch_shapes=[
            pltpu.VMEM((ts + V7X_SUBLANES, D_RNN), F32),
            pltpu.VMEM((ts, D_RNN), F32),
            pltpu.VMEM((ts, D_RNN), F32),
            pltpu.VMEM((ts, D_RNN), F32),
            pltpu.VMEM((1, D_RNN), F32),
            pltpu.VMEM((WINDOW + ts, KV_W), BF16),
            pltpu.VMEM((N_KV_HEADS, V_ROWS, WINDOW + ts), BF16),
            pltpu.VMEM((Q_W, ts), F32),
        ],
    )
    return pl.pallas_call(
        _mixer_kernel,
        grid_spec=grid_spec,
        out_shape=jax.ShapeDtypeStruct(x.shape, F32),
        compiler_params=pltpu.CompilerParams(
            dimension_semantics=("arbitrary", "arbitrary"), vmem_limit_bytes=VMEM_LIMIT),
        name="mixer",
    )(sinks2, x, w_in, g_mix, conv_w, conv_b, wgate, hb_rg, hb_ig, lam, w_rnn, rope, fold,
      w_attn, w_out)


def _ffn(x2d, p2d, g_mlp, w_up, w_down, g_ple, w_pg, w_pp):
    rows = x2d.shape[0]
    tm = FFN_ROWS
    assert rows % tm == 0
    return pl.pallas_call(
        _ffn_kernel,
        grid=(rows // tm,),
        in_specs=[
            pl.BlockSpec((tm, D_MODEL), lambda i: (i, 0)),
            pl.BlockSpec((tm, PLE_DIM), lambda i: (i, 0)),
            _resident(g_mlp.shape), _resident(w_up.shape), _resident(w_down.shape),
            _resident(g_ple.shape), _resident(w_pg.shape), _resident(w_pp.shape),
        ],
        out_specs=pl.BlockSpec((tm, D_MODEL), lambda i: (i, 0)),
        out_shape=jax.ShapeDtypeStruct(x2d.shape, F32),
        compiler_params=pltpu.CompilerParams(
            dimension_semantics=("arbitrary",), vmem_limit_bytes=VMEM_LIMIT),
        name="ffn",
    )(x2d, p2d, g_mlp, w_up, w_down, g_ple, w_pg, w_pp)


def kernel(x, p, g_mix, w_in, conv_w, conv_b, w_rg, b_rg, w_ig, b_ig, lru_lambda, w_rnn_proj, q_gain, k_gain, sinks, w_attn_proj, w_out, g_mlp, w_up, w_down, g_ple, w_ple_gate, w_ple_proj):
    batch, seq, _ = x.shape
    depth = p.shape[0]
    fold = _fold_masks()
    in_scale = jnp.asarray(np.where(np.arange(IN_TOTAL) >= OFF_GA, 0.5, 1.0), F32)
    row = lambda v: v.reshape(1, -1)
    for l in range(depth):
        rope = jnp.stack(_rope_gain_tables(seq, q_gain[l], LOG2E * HEAD_DIM ** -0.5)
                         + _rope_gain_tables(seq, k_gain[l], 1.0))
        w_in_b = (w_in[l] * in_scale).astype(BF16)
        wgate = jnp.concatenate([_block_diag_tiles(w_rg[l]), _block_diag_tiles(w_ig[l])], axis=2).astype(BF16)
        mats = (w_in_b, wgate, (0.5 * w_rnn_proj[l]).astype(BF16),
                w_attn_proj[l].astype(BF16), (0.5 * w_out[l]).astype(BF16))
        vecs = (row(g_mix[l]), 0.5 * conv_w[l], row(0.5 * conv_b[l]), row(0.5 * b_rg[l]), row(0.5 * b_ig[l]),
                row(lru_lambda[l]), rope, fold)
        x = _mixer(x, sinks[l] * LOG2E, mats, vecs)
        x = _ffn(x.reshape(batch * seq, D_MODEL), p[l].reshape(batch * seq, PLE_DIM), row(g_mlp[l]),
                 w_up[l].astype(BF16), w_down[l].astype(BF16), row(g_ple[l]),
                 (0.5 * w_ple_gate[l]).astype(BF16),
                 (0.5 * w_ple_proj[l]).astype(BF16)).reshape(batch, seq, D_MODEL)
    return x
```

```python
import math

import jax
import jax.numpy as jnp
import numpy as np
from jax import lax
from jax.experimental import pallas as pl
from jax.experimental.pallas import tpu as pltpu

D_MODEL = 1024
D_RNN = D_MODEL
RNN_BLOCKS = 16
RNN_BLOCK_W = D_RNN // RNN_BLOCKS
CONV_W = 4
LRU_C = 8.0
HEAD_DIM = 64
HALF_DIM = HEAD_DIM // 2
N_Q_HEADS = D_MODEL // HEAD_DIM
N_KV_HEADS = 4
GQA_GROUP = N_Q_HEADS // N_KV_HEADS
WINDOW = 128
ROPE_THETA = 10000.0
Q_W = N_Q_HEADS * HEAD_DIM
KV_W = N_KV_HEADS * HEAD_DIM
QKV_W = Q_W + 2 * KV_W
D_FF = 4 * D_MODEL
PLE_DIM = 256
NORM_EPS = 1e-6
LOG2E = math.log2(math.e)
GELU_C0 = math.sqrt(2.0 / math.pi)
GELU_C1 = GELU_C0 * 0.044715

OFF_XR = 0
OFF_GR = OFF_XR + D_RNN
OFF_Q = OFF_GR + D_RNN
OFF_GA = OFF_Q + QKV_W
OFF_GB = OFF_GA + D_MODEL
IN_TOTAL = OFF_GB + D_MODEL

V7X_SUBLANES = 8
V7X_BF16_SUBLANES = 16
V7X_MXU_DIM = 256
V7X_VMEM_BYTES = 64 * 1024 * 1024

GATE_TILES = D_RNN // V7X_MXU_DIM
QKV_TILES = QKV_W // V7X_MXU_DIM
SEQ_CHUNK = 512
FFN_ROWS = 1024
FF_CHUNK = 1024
V_ROWS = HEAD_DIM + V7X_BF16_SUBLANES
VMEM_LIMIT = V7X_VMEM_BYTES * 3 // 4

BF16 = jnp.bfloat16
F32 = jnp.float32


def _dot(a, b):
    return jnp.dot(a, b, preferred_element_type=F32)


def _dot_tn(a, b):
    return lax.dot_general(a, b, (((0,), (0,)), ((), ())), preferred_element_type=F32)


def _rmsnorm(t, g):
    ms = jnp.mean(t * t, axis=-1, keepdims=True)
    return t * lax.rsqrt(ms + NORM_EPS) * g


def _head_norm_rope_t(z_t, n_heads, tabs):
    c_lo, s_hi, c_hi, s_lo = tabs
    outs = []
    for hd in range(n_heads):
        t = z_t[hd * HEAD_DIM:(hd + 1) * HEAD_DIM]
        inv = lax.rsqrt(jnp.sum(t * t, axis=0, keepdims=True) * (1.0 / HEAD_DIM) + NORM_EPS)
        t1, t2 = t[:HALF_DIM], t[HALF_DIM:]
        outs.append((t1 * c_lo - t2 * s_hi) * inv)
        outs.append((t2 * c_hi + t1 * s_lo) * inv)
    return jnp.concatenate(outs, axis=0)


def _mixer_kernel(sinks2_ref, x_ref, w_in_ref, g_mix_ref, conv_w_ref, conv_b_ref, wgate_ref,
                  hb_rg_ref, hb_ig_ref, lam_ref, w_rnn_ref, rope_ref, fold_ref, w_attn_ref, w_out_ref,
                  o_ref,
                  ext_ref, xc_ref, a_ref, b_ref, hcar_ref, kext_ref, vext_ref, ao_ref):
    ts = SEQ_CHUNK
    j = pl.program_id(1)

    @pl.when(j == 0)
    def _():
        ext_ref[0:V7X_SUBLANES, :] = jnp.zeros((V7X_SUBLANES, D_RNN), F32)
        hcar_ref[...] = jnp.zeros_like(hcar_ref)
        kext_ref[0:WINDOW, :] = jnp.zeros((WINDOW, KV_W), BF16)
        vext_ref[:, 0:HEAD_DIM, 0:WINDOW] = jnp.zeros((N_KV_HEADS, HEAD_DIM, WINDOW), BF16)
        vext_ref[:, HEAD_DIM:V_ROWS, :] = jnp.ones((N_KV_HEADS, V_ROWS - HEAD_DIM, WINDOW + ts), BF16)

    h = _rmsnorm(x_ref[0], g_mix_ref[...]).astype(BF16)

    def proj(lo, width):
        return _dot(h, w_in_ref[:, lo:lo + width])

    ext_ref[V7X_SUBLANES:V7X_SUBLANES + ts, :] = proj(OFF_XR, D_RNN)
    cw = [conv_w_ref[jj:jj + 1, :] for jj in range(CONV_W)]
    cb = conv_b_ref[...]
    taps = [ext_ref[V7X_SUBLANES - CONV_W + 1 + jj:V7X_SUBLANES - CONV_W + 2 + jj, :]
            for jj in range(CONV_W - 1)]
    for t in range(ts):
        taps.append(ext_ref[V7X_SUBLANES + t:V7X_SUBLANES + t + 1, :])
        acc = cb + taps[0] * cw[0]
        for jj in range(1, CONV_W):
            acc = acc + taps[jj] * cw[jj]
        xc_ref[t:t + 1, :] = acc
        taps = taps[1:]
    ext_ref[0:V7X_SUBLANES, :] = ext_ref[ts:ts + V7X_SUBLANES, :]

    qkv_t = [proj(OFF_Q + tile * V7X_MXU_DIM, V7X_MXU_DIM).T for tile in range(QKV_TILES)]

    lam = lam_ref[...]
    half_c_sp = (0.5 * LRU_C) * (jnp.maximum(-lam, 0.0) + jnp.log1p(jnp.exp(-jnp.abs(lam))))
    pos = pl.ds(pl.multiple_of(j * ts, ts), ts)
    q_tabs = [rope_ref[t, :, pos] for t in range(4)]
    k_tabs = [rope_ref[4 + t, :, pos] for t in range(4)]
    k_tile = qkv_t[Q_W // V7X_MXU_DIM]
    gelu_parts, ta_parts, tb_parts, q_parts, k_parts = [], [], [], [], []
    for c in range(GATE_TILES):
        sl = slice(c * V7X_MXU_DIM, (c + 1) * V7X_MXU_DIM)
        xh = xc_ref[:, sl]
        gz = _dot(xh.astype(BF16), wgate_ref[c])
        t_r = jnp.tanh(gz[:, :V7X_MXU_DIM] + hb_rg_ref[:, sl])
        t_i = jnp.tanh(gz[:, V7X_MXU_DIM:] + hb_ig_ref[:, sl])
        hcs = half_c_sp[:, sl]
        neg_log_a = hcs + hcs * t_r
        a = jnp.exp2(neg_log_a * (-LOG2E))
        a_ref[:, sl] = a
        y = jnp.tanh(neg_log_a) * (a * a + 1.0)
        b_ref[:, sl] = jnp.where(y > 0.0, y * lax.rsqrt(y), 0.0) * (xh + xh * t_i)

        gr = proj(OFF_GR + c * V7X_MXU_DIM, V7X_MXU_DIM)
        gelu_parts.append((gr, jnp.tanh(gr * (GELU_C0 + GELU_C1 * (gr * gr)))))
        ta_parts.append(jnp.tanh(proj(OFF_GA + c * V7X_MXU_DIM, V7X_MXU_DIM)))
        tb_parts.append(jnp.tanh(proj(OFF_GB + c * V7X_MXU_DIM, V7X_MXU_DIM)))
        q_parts.append(_head_norm_rope_t(qkv_t[c], V7X_MXU_DIM // HEAD_DIM, q_tabs).astype(BF16))
        k_parts.append(_head_norm_rope_t(k_tile[c * HEAD_DIM:(c + 1) * HEAD_DIM], 1, k_tabs))
    q_t = jnp.concatenate(q_parts, axis=0)
    kext_ref[WINDOW:WINDOW + ts, :] = jnp.concatenate(k_parts, axis=0).T.astype(BF16)
    v_t = qkv_t[(Q_W + KV_W) // V7X_MXU_DIM].astype(BF16)
    for g in range(N_KV_HEADS):
        vext_ref[g, 0:HEAD_DIM, WINDOW:WINDOW + ts] = v_t[g * HEAD_DIM:(g + 1) * HEAD_DIM]

    hc = hcar_ref[...]
    for t in range(ts):
        hc = a_ref[t:t + 1, :] * hc + b_ref[t:t + 1, :]
        b_ref[t:t + 1, :] = hc
    hcar_ref[...] = hc
    ya_in = []
    for c in range(GATE_TILES):
        gr, tg = gelu_parts[c]
        u = b_ref[:, c * V7X_MXU_DIM:(c + 1) * V7X_MXU_DIM] * gr
        ya_in.append((u + u * tg).astype(BF16))
    ya_in = jnp.concatenate(ya_in, axis=1)

    fold_shape = (WINDOW, GQA_GROUP * WINDOW)
    old_key = (lax.broadcasted_iota(jnp.int32, fold_shape, 0)
               > (lax.broadcasted_iota(jnp.int32, fold_shape, 1) & (WINDOW - 1)))
    no_history = jnp.where(j > 0, 0.0, -jnp.inf).astype(F32)

    def scores(n, g):
        cols = slice(n * WINDOW, (n + 1) * WINDOW)
        q_g = jnp.concatenate([q_t[hd * HEAD_DIM:(hd + 1) * HEAD_DIM, cols]
                               for hd in range(g * GQA_GROUP, (g + 1) * GQA_GROUP)], axis=1)
        return _dot(kext_ref[n * WINDOW:(n + 2) * WINDOW, g * HEAD_DIM:(g + 1) * HEAD_DIM], q_g)

    def softmax(n, g, s_t):
        s_old, s_new = s_t[0:WINDOW], s_t[WINDOW:2 * WINDOW]
        if n == 0:
            s_old = s_old + no_history
        s_win = jnp.where(old_key, s_old, s_new)
        probs, ms = [], []
        for i in range(GQA_GROUP):
            s_i = s_win[:, i * WINDOW:(i + 1) * WINDOW]
            m = jnp.maximum(jnp.max(s_i, axis=0, keepdims=True), sinks2_ref[g * GQA_GROUP + i])
            ms.append(m)
            probs.append(jnp.exp2(s_i - m).astype(BF16))
        p_win = jnp.concatenate(probs, axis=1)
        return jnp.concatenate([p_win * fold_ref[0], p_win * fold_ref[1]], axis=0), ms

    def weighted_values(n, g, p_band, ms):
        cols = slice(n * WINDOW, (n + 1) * WINDOW)
        o_t = _dot(vext_ref[g, :, n * WINDOW:(n + 2) * WINDOW], p_band)
        for i in range(GQA_GROUP):
            hd = g * GQA_GROUP + i
            blk = slice(i * WINDOW, (i + 1) * WINDOW)
            den = o_t[HEAD_DIM:HEAD_DIM + 1, blk] + jnp.exp2(sinks2_ref[hd] - ms[i])
            ao_ref[hd * HEAD_DIM:(hd + 1) * HEAD_DIM, cols] = o_t[0:HEAD_DIM, blk] / den

    work = [(n, g) for n in range(ts // WINDOW) for g in range(N_KV_HEADS)]
    s_q, p_q = {}, {}
    for step in range(len(work) + 2):
        if step < len(work):
            s_q[step] = scores(*work[step])
        if 0 <= step - 1 < len(work):
            p_q[step - 1] = softmax(*work[step - 1], s_q.pop(step - 1))
        if 0 <= step - 2 < len(work):
            weighted_values(*work[step - 2], *p_q.pop(step - 2))
    kext_ref[0:WINDOW, :] = kext_ref[ts:ts + WINDOW, :]
    vext_ref[:, 0:HEAD_DIM, 0:WINDOW] = vext_ref[:, 0:HEAD_DIM, ts:ts + WINDOW]

    y_a = _dot(ya_in, w_rnn_ref[...])
    y_b = _dot_tn(ao_ref[...].astype(BF16), w_attn_ref[...])
    t_a = jnp.concatenate(ta_parts, axis=1)
    t_b = jnp.concatenate(tb_parts, axis=1)
    merged = ((y_a + t_a * y_a) + (y_b + t_b * y_b)).astype(BF16)
    o_ref[0] = x_ref[0] + _dot(merged, w_out_ref[...])


def _ffn_kernel(x_ref, p_ref, g_mlp_ref, w_up_ref, w_down_ref, g_ple_ref, w_pg_ref, w_pp_ref, o_ref):
    x = x_ref[...]
    hm = _rmsnorm(x, g_mlp_ref[...]).astype(BF16)
    acc = x
    for c in range(D_FF // FF_CHUNK):
        u = jnp.maximum(_dot(hm, w_up_ref[:, c * FF_CHUNK:(c + 1) * FF_CHUNK]), 0.0)
        acc = acc + _dot((u * u).astype(BF16), w_down_ref[c * FF_CHUNK:(c + 1) * FF_CHUNK, :])
    hp = _rmsnorm(acc, g_ple_ref[...]).astype(BF16)
    pb = p_ref[...].astype(BF16)
    for c in range(D_MODEL // V7X_MXU_DIM):
        sl = slice(c * V7X_MXU_DIM, (c + 1) * V7X_MXU_DIM)
        t_g = jnp.tanh(_dot(hp, w_pg_ref[:, sl]))
        e_h = _dot(pb, w_pp_ref[:, sl])
        o_ref[:, sl] = acc[:, sl] + (e_h + e_h * t_g)


def _resident(shape):
    nd = len(shape)
    return pl.BlockSpec(shape, lambda *_: (0,) * nd, pipeline_mode=pl.Buffered(1))


def _block_diag_tiles(w):
    per = V7X_MXU_DIM // RNN_BLOCK_W
    w4 = w.reshape(GATE_TILES, per, RNN_BLOCK_W, RNN_BLOCK_W)
    rows = [jnp.pad(w4[:, g], ((0, 0), (0, 0), (g * RNN_BLOCK_W, (per - 1 - g) * RNN_BLOCK_W)))
            for g in range(per)]
    return jnp.concatenate(rows, axis=1)


def _rope_gain_tables(seq, gain, scale):
    inv = ROPE_THETA ** (-np.arange(0, HEAD_DIM, 2, dtype=np.float64) / HEAD_DIM)
    ang = inv[:, None] * np.arange(seq, dtype=np.float64)[None, :]
    cos, sin = jnp.asarray(np.cos(ang), F32), jnp.asarray(np.sin(ang), F32)
    g_lo = (gain[:HALF_DIM] * scale)[:, None]
    g_hi = (gain[HALF_DIM:] * scale)[:, None]
    return [g_lo * cos, g_hi * sin, g_hi * cos, g_lo * sin]


def _fold_masks():
    older = np.arange(WINDOW)[:, None] > np.arange(WINDOW)[None, :]
    both = np.stack([older, ~older]).astype(np.float32)
    return jnp.asarray(np.tile(both, (1, 1, GQA_GROUP)), BF16)


def _mixer(x, sinks2, mats, vecs):
    batch, seq, _ = x.shape
    ts = SEQ_CHUNK
    assert seq % ts == 0 and ts % WINDOW == 0
    assert GATE_TILES == N_KV_HEADS == Q_W // V7X_MXU_DIM
    (w_in, wgate, w_rnn, w_attn, w_out) = mats
    (g_mix, conv_w, conv_b, hb_rg, hb_ig, lam, rope, fold) = vecs
    in_specs = [
        pl.BlockSpec((1, ts, D_MODEL), lambda b, j, *_: (b, j, 0)),
        _resident(w_in.shape), _resident(g_mix.shape), _resident(conv_w.shape),
        _resident(conv_b.shape), _resident(wgate.shape), _resident(hb_rg.shape), _resident(hb_ig.shape),
        _resident(lam.shape), _resident(w_rnn.shape),
        _resident(rope.shape),
        _resident(fold.shape), _resident(w_attn.shape), _resident(w_out.shape),
    ]
    grid_spec = pltpu.PrefetchScalarGridSpec(
        num_scalar_prefetch=1,
        grid=(batch, seq // ts),
        in_specs=in_specs,
        out_specs=pl.BlockSpec((1, ts, D_MODEL), lambda b, j, *_: (b, j, 0)),
        scratch_shapes=[
            pltpu.VMEM((ts + V7X_SUBLANES, D_RNN), F32),
            pltpu.VMEM((ts, D_RNN), F32),
            pltpu.VMEM((ts, D_RNN), F32),
            pltpu.VMEM((ts, D_RNN), F32),
            pltpu.VMEM((1, D_RNN), F32),
            pltpu.VMEM((WINDOW + ts, KV_W), BF16),
            pltpu.VMEM((N_KV_HEADS, V_ROWS, WINDOW + ts), BF16),
            pltpu.VMEM((Q_W, ts), F32),
        ],
    )
    return pl.pallas_call(
        _mixer_kernel,
        grid_spec=grid_spec,
        out_shape=jax.ShapeDtypeStruct(x.shape, F32),
        compiler_params=pltpu.CompilerParams(
            dimension_semantics=("arbitrary", "arbitrary"), vmem_limit_bytes=VMEM_LIMIT),
        name="mixer",
    )(sinks2, x, w_in, g_mix, conv_w, conv_b, wgate, hb_rg, hb_ig, lam, w_rnn, rope, fold,
      w_attn, w_out)


def _ffn(x2d, p2d, g_mlp, w_up, w_down, g_ple, w_pg, w_pp):
    rows = x2d.shape[0]
    tm = FFN_ROWS
    assert rows % tm == 0
    return pl.pallas_call(
        _ffn_kernel,
        grid=(rows // tm,),
        in_specs=[
            pl.BlockSpec((tm, D_MODEL), lambda i: (i, 0)),
            pl.BlockSpec((tm, PLE_DIM), lambda i: (i, 0)),
            _resident(g_mlp.shape), _resident(w_up.shape), _resident(w_down.shape),
            _resident(g_ple.shape), _resident(w_pg.shape), _resident(w_pp.shape),
        ],
        out_specs=pl.BlockSpec((tm, D_MODEL), lambda i: (i, 0)),
        out_shape=jax.ShapeDtypeStruct(x2d.shape, F32),
        compiler_params=pltpu.CompilerParams(
            dimension_semantics=("arbitrary",), vmem_limit_bytes=VMEM_LIMIT),
        name="ffn",
    )(x2d, p2d, g_mlp, w_up, w_down, g_ple, w_pg, w_pp)


def kernel(x, p, g_mix, w_in, conv_w, conv_b, w_rg, b_rg, w_ig, b_ig, lru_lambda, w_rnn_proj, q_gain, k_gain, sinks, w_attn_proj, w_out, g_mlp, w_up, w_down, g_ple, w_ple_gate, w_ple_proj):
    batch, seq, _ = x.shape
    depth = p.shape[0]
    fold = _fold_masks()
    in_scale = jnp.asarray(np.where(np.arange(IN_TOTAL) >= OFF_GA, 0.5, 1.0), F32)
    row = lambda v: v.reshape(1, -1)
    for l in range(depth):
        rope = jnp.stack(_rope_gain_tables(seq, q_gain[l], LOG2E * HEAD_DIM ** -0.5)
                         + _rope_gain_tables(seq, k_gain[l], 1.0))
        w_in_b = (w_in[l] * in_scale).astype(BF16)
        wgate = jnp.concatenate([_block_diag_tiles(w_rg[l]), _block_diag_tiles(w_ig[l])], axis=2).astype(BF16)
        mats = (w_in_b, wgate, (0.5 * w_rnn_proj[l]).astype(BF16),
                w_attn_proj[l].astype(BF16), (0.5 * w_out[l]).astype(BF16))
        vecs = (row(g_mix[l]), 0.5 * conv_w[l], row(0.5 * conv_b[l]), row(0.5 * b_rg[l]), row(0.5 * b_ig[l]),
                row(lru_lambda[l]), rope, fold)
        x = _mixer(x, sinks[l] * LOG2E, mats, vecs)
        x = _ffn(x.reshape(batch * seq, D_MODEL), p[l].reshape(batch * seq, PLE_DIM), row(g_mlp[l]),
                 w_up[l].astype(BF16), w_down[l].astype(BF16), row(g_ple[l]),
                 (0.5 * w_ple_gate[l]).astype(BF16),
                 (0.5 * w_ple_proj[l]).astype(BF16)).reshape(batch, seq, D_MODEL)
    return x
```

```python
import math

import jax
import jax.numpy as jnp
import numpy as np
from jax import lax
from jax.experimental import pallas as pl
from jax.experimental.pallas import tpu as pltpu

D_MODEL = 1024
D_RNN = D_MODEL
RNN_BLOCKS = 16
RNN_BLOCK_W = D_RNN // RNN_BLOCKS
CONV_W = 4
LRU_C = 8.0
HEAD_DIM = 64
HALF_DIM = HEAD_DIM // 2
N_Q_HEADS = D_MODEL // HEAD_DIM
N_KV_HEADS = 4
GQA_GROUP = N_Q_HEADS // N_KV_HEADS
WINDOW = 128
ROPE_THETA = 10000.0
Q_W = N_Q_HEADS * HEAD_DIM
KV_W = N_KV_HEADS * HEAD_DIM
QKV_W = Q_W + 2 * KV_W
D_FF = 4 * D_MODEL
PLE_DIM = 256
NORM_EPS = 1e-6
LOG2E = math.log2(math.e)
GELU_C0 = math.sqrt(2.0 / math.pi)
GELU_C1 = GELU_C0 * 0.044715

OFF_XR = 0
OFF_GR = OFF_XR + D_RNN
OFF_Q = OFF_GR + D_RNN
OFF_GA = OFF_Q + QKV_W
OFF_GB = OFF_GA + D_MODEL
IN_TOTAL = OFF_GB + D_MODEL

V7X_LANES = 128
V7X_SUBLANES = 8
V7X_BF16_SUBLANES = 16
V7X_MXU_DIM = 256
V7X_VMEM_BYTES = 64 * 1024 * 1024

GATE_TILES = D_RNN // V7X_MXU_DIM
QKV_TILES = QKV_W // V7X_MXU_DIM
SEQ_CHUNK = 512
FFN_ROWS = 1024
FF_CHUNK = 1024
V_ROWS = HEAD_DIM + V7X_BF16_SUBLANES
VMEM_LIMIT = V7X_VMEM_BYTES * 3 // 4

BF16 = jnp.bfloat16
F32 = jnp.float32


def _dot(a, b):
    return jnp.dot(a, b, preferred_element_type=F32)


def _dot_tn(a, b):
    return lax.dot_general(a, b, (((0,), (0,)), ((), ())), preferred_element_type=F32)


def _rmsnorm(t, g):
    ms = jnp.mean(t * t, axis=-1, keepdims=True)
    return t * lax.rsqrt(ms + NORM_EPS) * g


def _head_norm_rope_t(z_t, n_heads, tabs):
    c_lo, s_hi, c_hi, s_lo = tabs
    outs = []
    for hd in range(n_heads):
        t = z_t[hd * HEAD_DIM:(hd + 1) * HEAD_DIM]
        inv = lax.rsqrt(jnp.sum(t * t, axis=0, keepdims=True) * (1.0 / HEAD_DIM) + NORM_EPS)
        t1, t2 = t[:HALF_DIM], t[HALF_DIM:]
        outs.append((t1 * c_lo - t2 * s_hi) * inv)
        outs.append((t2 * c_hi + t1 * s_lo) * inv)
    return jnp.concatenate(outs, axis=0)


def _mixer_kernel(sinks2_ref, x_ref, w_in_ref, g_mix_ref, conv_w_ref, conv_b_ref, wgate_ref,
                  hb_rg_ref, hb_ig_ref, lam_ref, w_rnn_ref, rope_ref, fold_ref, w_attn_ref, w_out_ref,
                  o_ref,
                  ext_ref, xc_ref, a_ref, b_ref, hcar_ref, kext_ref, vext_ref, ao_ref):
    ts = SEQ_CHUNK
    j = pl.program_id(1)

    @pl.when(j == 0)
    def _():
        ext_ref[0:V7X_SUBLANES, :] = jnp.zeros((V7X_SUBLANES, D_RNN), F32)
        hcar_ref[...] = jnp.zeros_like(hcar_ref)
        kext_ref[0:WINDOW, :] = jnp.zeros((WINDOW, KV_W), BF16)
        vext_ref[:, 0:HEAD_DIM, 0:WINDOW] = jnp.zeros((N_KV_HEADS, HEAD_DIM, WINDOW), BF16)
        vext_ref[:, HEAD_DIM:V_ROWS, :] = jnp.ones((N_KV_HEADS, V_ROWS - HEAD_DIM, WINDOW + ts), BF16)

    h = _rmsnorm(x_ref[0], g_mix_ref[...]).astype(BF16)

    def proj(lo, width):
        return _dot(h, w_in_ref[:, lo:lo + width])

    ext_ref[V7X_SUBLANES:V7X_SUBLANES + ts, :] = proj(OFF_XR, D_RNN)
    cw = [conv_w_ref[jj:jj + 1, :] for jj in range(CONV_W)]
    cb = conv_b_ref[...]
    taps = [ext_ref[V7X_SUBLANES - CONV_W + 1 + jj:V7X_SUBLANES - CONV_W + 2 + jj, :]
            for jj in range(CONV_W - 1)]
    for t in range(ts):
        taps.append(ext_ref[V7X_SUBLANES + t:V7X_SUBLANES + t + 1, :])
        acc = cb + taps[0] * cw[0]
        for jj in range(1, CONV_W):
            acc = acc + taps[jj] * cw[jj]
        xc_ref[t:t + 1, :] = acc
        taps = taps[1:]
    ext_ref[0:V7X_SUBLANES, :] = ext_ref[ts:ts + V7X_SUBLANES, :]

    qkv_t = [proj(OFF_Q + tile * V7X_MXU_DIM, V7X_MXU_DIM).T for tile in range(QKV_TILES)]

    lam = lam_ref[...]
    half_c_sp = (0.5 * LRU_C) * (jnp.maximum(-lam, 0.0) + jnp.log1p(jnp.exp(-jnp.abs(lam))))
    pos = pl.ds(pl.multiple_of(j * ts, ts), ts)
    q_tabs = [rope_ref[t, :, pos] for t in range(4)]
    k_tabs = [rope_ref[4 + t, :, pos] for t in range(4)]
    k_tile = qkv_t[Q_W // V7X_MXU_DIM]
    gelu_parts, ta_parts, tb_parts, q_parts, k_parts = [], [], [], [], []
    for c in range(GATE_TILES):
        sl = slice(c * V7X_MXU_DIM, (c + 1) * V7X_MXU_DIM)
        xh = xc_ref[:, sl]
        gz = _dot(xh.astype(BF16), wgate_ref[c])
        t_r = jnp.tanh(gz[:, :V7X_MXU_DIM] + hb_rg_ref[:, sl])
        t_i = jnp.tanh(gz[:, V7X_MXU_DIM:] + hb_ig_ref[:, sl])
        hcs = half_c_sp[:, sl]
        neg_log_a = hcs + hcs * t_r
        a = jnp.exp2(neg_log_a * (-LOG2E))
        a_ref[:, sl] = a
        y = jnp.tanh(neg_log_a) * (a * a + 1.0)
        b_ref[:, sl] = jnp.where(y > 0.0, y * lax.rsqrt(y), 0.0) * (xh + xh * t_i)

        gr = proj(OFF_GR + c * V7X_MXU_DIM, V7X_MXU_DIM)
        gelu_parts.append((gr, jnp.tanh(gr * (GELU_C0 + GELU_C1 * (gr * gr)))))
        ta_parts.append(jnp.tanh(proj(OFF_GA + c * V7X_MXU_DIM, V7X_MXU_DIM)))
        tb_parts.append(jnp.tanh(proj(OFF_GB + c * V7X_MXU_DIM, V7X_MXU_DIM)))
        q_parts.append(_head_norm_rope_t(qkv_t[c], V7X_MXU_DIM // HEAD_DIM, q_tabs).astype(BF16))
        k_parts.append(_head_norm_rope_t(k_tile[c * HEAD_DIM:(c + 1) * HEAD_DIM], 1, k_tabs))
    q_t = jnp.concatenate(q_parts, axis=0)
    kext_ref[WINDOW:WINDOW + ts, :] = jnp.concatenate(k_parts, axis=0).T.astype(BF16)
    v_t = qkv_t[(Q_W + KV_W) // V7X_MXU_DIM].astype(BF16)
    for g in range(N_KV_HEADS):
        vext_ref[g, 0:HEAD_DIM, WINDOW:WINDOW + ts] = v_t[g * HEAD_DIM:(g + 1) * HEAD_DIM]

    hc = hcar_ref[...]
    for t in range(ts):
        hc = a_ref[t:t + 1, :] * hc + b_ref[t:t + 1, :]
        b_ref[t:t + 1, :] = hc
    hcar_ref[...] = hc
    ya_in = []
    for c in range(GATE_TILES):
        gr, tg = gelu_parts[c]
        u = b_ref[:, c * V7X_MXU_DIM:(c + 1) * V7X_MXU_DIM] * gr
        ya_in.append((u + u * tg).astype(BF16))
    ya_in = jnp.concatenate(ya_in, axis=1)

    fold_shape = (WINDOW, GQA_GROUP * WINDOW)
    old_key = (lax.broadcasted_iota(jnp.int32, fold_shape, 0)
               > (lax.broadcasted_iota(jnp.int32, fold_shape, 1) & (WINDOW - 1)))
    no_history = jnp.where(j > 0, 0.0, -jnp.inf).astype(F32)

    def scores(n, g):
        cols = slice(n * WINDOW, (n + 1) * WINDOW)
        q_g = jnp.concatenate([q_t[hd * HEAD_DIM:(hd + 1) * HEAD_DIM, cols]
                               for hd in range(g * GQA_GROUP, (g + 1) * GQA_GROUP)], axis=1)
        return _dot(kext_ref[n * WINDOW:(n + 2) * WINDOW, g * HEAD_DIM:(g + 1) * HEAD_DIM], q_g)

    def softmax(n, g, s_t):
        s_old, s_new = s_t[0:WINDOW], s_t[WINDOW:2 * WINDOW]
        if n == 0:
            s_old = s_old + no_history
        s_win = jnp.where(old_key, s_old, s_new)
        probs, ms = [], []
        for i in range(GQA_GROUP):
            s_i = s_win[:, i * WINDOW:(i + 1) * WINDOW]
            m = jnp.maximum(jnp.max(s_i, axis=0, keepdims=True), sinks2_ref[g * GQA_GROUP + i])
            ms.append(m)
            probs.append(jnp.exp2(s_i - m).astype(BF16))
        p_win = jnp.concatenate(probs, axis=1)
        return jnp.concatenate([p_win * fold_ref[0], p_win * fold_ref[1]], axis=0), ms

    def weighted_values(n, g, p_band, ms):
        cols = slice(n * WINDOW, (n + 1) * WINDOW)
        o_t = _dot(vext_ref[g, :, n * WINDOW:(n + 2) * WINDOW], p_band)
        for i in range(GQA_GROUP):
            hd = g * GQA_GROUP + i
            blk = slice(i * WINDOW, (i + 1) * WINDOW)
            den = o_t[HEAD_DIM:HEAD_DIM + 1, blk] + jnp.exp2(sinks2_ref[hd] - ms[i])
            ao_ref[hd * HEAD_DIM:(hd + 1) * HEAD_DIM, cols] = o_t[0:HEAD_DIM, blk] / den

    work = [(n, g) for n in range(ts // WINDOW) for g in range(N_KV_HEADS)]
    s_q, p_q = {}, {}
    for step in range(len(work) + 2):
        if step < len(work):
            s_q[step] = scores(*work[step])
        if 0 <= step - 1 < len(work):
            p_q[step - 1] = softmax(*work[step - 1], s_q.pop(step - 1))
        if 0 <= step - 2 < len(work):
            weighted_values(*work[step - 2], *p_q.pop(step - 2))
    kext_ref[0:WINDOW, :] = kext_ref[ts:ts + WINDOW, :]
    vext_ref[:, 0:HEAD_DIM, 0:WINDOW] = vext_ref[:, 0:HEAD_DIM, ts:ts + WINDOW]

    y_a = _dot(ya_in, w_rnn_ref[...])
    y_b = _dot_tn(ao_ref[...].astype(BF16), w_attn_ref[...])
    t_a = jnp.concatenate(ta_parts, axis=1)
    t_b = jnp.concatenate(tb_parts, axis=1)
    merged = ((y_a + t_a * y_a) + (y_b + t_b * y_b)).astype(BF16)
    o_ref[0] = x_ref[0] + _dot(merged, w_out_ref[...])


def _ffn_kernel(x_ref, p_ref, g_mlp_ref, w_up_ref, w_down_ref, g_ple_ref, w_pg_ref, w_pp_ref, o_ref,
                hm_ref, acc_ref):
    c = pl.program_id(1)

    @pl.when(c == 0)
    def _():
        x = x_ref[...]
        hm_ref[...] = _rmsnorm(x, g_mlp_ref[...]).astype(BF16)
        acc_ref[...] = x

    u = jnp.maximum(_dot(hm_ref[...], w_up_ref[...]), 0.0)
    acc_ref[...] += _dot((u * u).astype(BF16), w_down_ref[...])

    @pl.when(c == D_FF // FF_CHUNK - 1)
    def _():
        acc = acc_ref[...]
        hp = _rmsnorm(acc, g_ple_ref[...]).astype(BF16)
        pb = p_ref[...].astype(BF16)
        for t in range(D_MODEL // V7X_MXU_DIM):
            sl = slice(t * V7X_MXU_DIM, (t + 1) * V7X_MXU_DIM)
            t_g = jnp.tanh(_dot(hp, w_pg_ref[:, sl]))
            e_h = _dot(pb, w_pp_ref[:, sl])
            o_ref[:, sl] = acc[:, sl] + (e_h + e_h * t_g)


def _resident(shape):
    nd = len(shape)
    return pl.BlockSpec(shape, lambda *_: (0,) * nd, pipeline_mode=pl.Buffered(1))


def _block_diag_tiles(w):
    per = V7X_MXU_DIM // RNN_BLOCK_W
    w4 = w.reshape(GATE_TILES, per, RNN_BLOCK_W, RNN_BLOCK_W)
    rows = [jnp.pad(w4[:, g], ((0, 0), (0, 0), (g * RNN_BLOCK_W, (per - 1 - g) * RNN_BLOCK_W)))
            for g in range(per)]
    return jnp.concatenate(rows, axis=1)


def _rope_gain_tables(seq, gain, scale):
    inv = ROPE_THETA ** (-np.arange(0, HEAD_DIM, 2, dtype=np.float64) / HEAD_DIM)
    ang = inv[:, None] * np.arange(seq, dtype=np.float64)[None, :]
    cos, sin = jnp.asarray(np.cos(ang), F32), jnp.asarray(np.sin(ang), F32)
    g_lo = (gain[:HALF_DIM] * scale)[:, None]
    g_hi = (gain[HALF_DIM:] * scale)[:, None]
    return [g_lo * cos, g_hi * sin, g_hi * cos, g_lo * sin]


def _fold_masks():
    older = np.arange(WINDOW)[:, None] > np.arange(WINDOW)[None, :]
    both = np.stack([older, ~older]).astype(np.float32)
    return jnp.asarray(np.tile(both, (1, 1, GQA_GROUP)), BF16)


def _mixer(x, sinks2, mats, vecs):
    batch, seq, _ = x.shape
    ts = SEQ_CHUNK
    assert seq % ts == 0 and ts % WINDOW == 0
    assert GATE_TILES == N_KV_HEADS == Q_W // V7X_MXU_DIM
    (w_in, wgate, w_rnn, w_attn, w_out) = mats
    (g_mix, conv_w, conv_b, hb_rg, hb_ig, lam, rope, fold) = vecs
    in_specs = [
        pl.BlockSpec((1, ts, D_MODEL), lambda b, j, *_: (b, j, 0)),
        _resident(w_in.shape), _resident(g_mix.shape), _resident(conv_w.shape),
        _resident(conv_b.shape), _resident(wgate.shape), _resident(hb_rg.shape), _resident(hb_ig.shape),
        _resident(lam.shape), _resident(w_rnn.shape),
        _resident(rope.shape),
        _resident(fold.shape), _resident(w_attn.shape), _resident(w_out.shape),
    ]
    grid_spec = pltpu.PrefetchScalarGridSpec(
        num_scalar_prefetch=1,
        grid=(batch, seq // ts),
        in_specs=in_specs,
        out_specs=pl.BlockSpec((1, ts, D_MODEL), lambda b, j, *_: (b, j, 0)),
        scratch_shapes=[
            pltpu.VMEM((ts + V7X_SUBLANES, D_RNN), F32),
            pltpu.VMEM((ts, D_RNN), F32),
            pltpu.VMEM((ts, D_RNN), F32),
            pltpu.VMEM((ts, D_RNN), F32),
            pltpu.VMEM((1, D_RNN), F32),
            pltpu.VMEM((WINDOW + ts, KV_W), BF16),
            pltpu.VMEM((N_KV_HEADS, V_ROWS, WINDOW + ts), BF16),
            pltpu.VMEM((Q_W, ts), F32),
        ],
    )
    return pl.pallas_call(
        _mixer_kernel,
        grid_spec=grid_spec,
        out_shape=jax.ShapeDtypeStruct(x.shape, F32),
        compiler_params=pltpu.CompilerParams(
            dimension_semantics=("arbitrary", "arbitrary"), vmem_limit_bytes=VMEM_LIMIT),
        name="mixer",
    )(sinks2, x, w_in, g_mix, conv_w, conv_b, wgate, hb_rg, hb_ig, lam, w_rnn, rope, fold,
      w_attn, w_out)


def _ffn(x2d, p2d, g_mlp, w_up, w_down, g_ple, w_pg, w_pp):
    rows = x2d.shape[0]
    tm = FFN_ROWS
    assert rows % tm == 0
    return pl.pallas_call(
        _ffn_kernel,
        grid=(rows // tm, D_FF // FF_CHUNK),
        in_specs=[
            pl.BlockSpec((tm, D_MODEL), lambda i, c: (i, 0)),
            pl.BlockSpec((tm, PLE_DIM), lambda i, c: (i, 0)),
            _resident(g_mlp.shape),
            pl.BlockSpec((D_MODEL, FF_CHUNK), lambda i, c: (0, c)),
            pl.BlockSpec((FF_CHUNK, D_MODEL), lambda i, c: (c, 0)),
            _resident(g_ple.shape), _resident(w_pg.shape), _resident(w_pp.shape),
        ],
        out_specs=pl.BlockSpec((tm, D_MODEL), lambda i, c: (i, 0)),
        out_shape=jax.ShapeDtypeStruct(x2d.shape, F32),
        scratch_shapes=[pltpu.VMEM((tm, D_MODEL), BF16), pltpu.VMEM((tm, D_MODEL), F32)],
        compiler_params=pltpu.CompilerParams(
            dimension_semantics=("arbitrary", "arbitrary"), vmem_limit_bytes=VMEM_LIMIT),
        name="ffn",
    )(x2d, p2d, g_mlp, w_up, w_down, g_ple, w_pg, w_pp)


def kernel(x, p, g_mix, w_in, conv_w, conv_b, w_rg, b_rg, w_ig, b_ig, lru_lambda, w_rnn_proj, q_gain, k_gain, sinks, w_attn_proj, w_out, g_mlp, w_up, w_down, g_ple, w_ple_gate, w_ple_proj):
    batch, seq, _ = x.shape
    depth = p.shape[0]
    fold = _fold_masks()
    in_scale = jnp.asarray(np.where(np.arange(IN_TOTAL) >= OFF_GA, 0.5, 1.0), F32)
    row = lambda v: v.reshape(1, -1)
    for l in range(depth):
        rope = jnp.stack(_rope_gain_tables(seq, q_gain[l], LOG2E * HEAD_DIM ** -0.5)
                         + _rope_gain_tables(seq, k_gain[l], 1.0))
        w_in_b = (w_in[l] * in_scale).astype(BF16)
        wgate = jnp.concatenate([_block_diag_tiles(w_rg[l]), _block_diag_tiles(w_ig[l])], axis=2).astype(BF16)
        mats = (w_in_b, wgate, (0.5 * w_rnn_proj[l]).astype(BF16),
                w_attn_proj[l].astype(BF16), (0.5 * w_out[l]).astype(BF16))
        vecs = (row(g_mix[l]), 0.5 * conv_w[l], row(0.5 * conv_b[l]), row(0.5 * b_rg[l]), row(0.5 * b_ig[l]),
                row(lru_lambda[l]), rope, fold)
        x = _mixer(x, sinks[l] * LOG2E, mats, vecs)
        x = _ffn(x.reshape(batch * seq, D_MODEL), p[l].reshape(batch * seq, PLE_DIM), row(g_mlp[l]),
                 w_up[l].astype(BF16), w_down[l].astype(BF16), row(g_ple[l]),
                 (0.5 * w_ple_gate[l]).astype(BF16),
                 (0.5 * w_ple_proj[l]).astype(BF16)).reshape(batch, seq, D_MODEL)
    return x
```
